```python
import math
import jax
import jax.numpy as jnp
from jax import lax
import numpy as np

D_MODEL = 1024
BATCH = 2
SEQ = 8192
DEPTH = 4
DEC_BATCH = 32
DEC_SEQ = 4
PAST_LEN = 8192
PAGE_SIZE = 128

HEAD_DIM = 64
H_A = 4
H_B = 6
H_C = 6
H_M = 4
N_MEM = 256
W_A = H_A * 2 * HEAD_DIM
W_B = H_B * HEAD_DIM
W_C = H_C * HEAD_DIM
W_M = H_M * HEAD_DIM
MIX = W_A + W_B + W_C + W_M
IN_GROUPS = (('a_q', W_A), ('a_k', W_A), ('a_v', W_A), ('a_g', W_A),
             ('b_q', W_B), ('b_k', W_B), ('b_v', W_B), ('b_g', W_B),
             ('c_q', W_C), ('c_k', W_C), ('c_v', W_C), ('c_g', W_C), ('c_f', H_C),
             ('m_q', W_M), ('m_g', W_M))
P_IN = sum(w for _, w in IN_GROUPS)
Q_BLOCK = 128
RET_CHUNK = 128
LN_EPS = 1e-5
NEG_INF = -1e30
FORGET_BIAS = 3.0
ALPHA = (2 * DEPTH) ** 0.25
BETA = (8 * DEPTH) ** -0.25

kernel_name = 'hybrid_diffattn_retention_fox_step'


def split_points():
    return [int(v) for v in np.cumsum([w for _, w in IN_GROUPS])[:-1]]


def alibi_slopes():
    return 2.0 ** (-8.0 * jnp.arange(1, H_A + 1, dtype=jnp.float32) / H_A)


def retention_log_decay():
    return jnp.log(1.0 - 2.0 ** (-5.0 - jnp.arange(H_B, dtype=jnp.float32)))


def layer_norm(x, g, b):
    xf = x.astype(jnp.float32)
    mu = jnp.mean(xf, -1, keepdims=True)
    var = jnp.mean(jnp.square(xf - mu), -1, keepdims=True)
    y = (xf - mu) * lax.rsqrt(var + LN_EPS) * g.astype(jnp.float32) + b.astype(jnp.float32)
    return y.astype(x.dtype)


def rms_norm(x, g=None):
    xf = x.astype(jnp.float32)
    y = xf * lax.rsqrt(jnp.mean(jnp.square(xf), -1, keepdims=True) + LN_EPS)
    if g is not None:
        y = y * g.astype(jnp.float32)
    return y.astype(x.dtype)


def weight_values(w, values):
    out, off = None, 0
    for v in values:
        n = v.shape[1]
        part = jnp.einsum('bhqk,bkhd->bqhd', w[..., off:off + n].astype(v.dtype), v)
        out = part if out is None else out + part
        off += n
    return out


def query_sweep(fn, q_arrays, q_pos):
    L = q_pos.shape[0]
    if L <= Q_BLOCK or L % Q_BLOCK:
        return fn(*q_arrays, q_pos)

    def one(start):
        qs = tuple(lax.dynamic_slice_in_dim(a, start, Q_BLOCK, axis=1) for a in q_arrays)
        return fn(*qs, lax.dynamic_slice_in_dim(q_pos, start, Q_BLOCK))

    out = lax.map(one, jnp.arange(0, L, Q_BLOCK, dtype=jnp.int32))
    out = jnp.moveaxis(out, 0, 1)
    return out.reshape(out.shape[0], L, *out.shape[3:])


def diff_attn_block(q, q_pos, segs, lam, slopes):
    scores = []
    for k, _, k_pos in segs:
        s = jnp.einsum('bqhcd,bkhcd->bhcqk', q, k, preferred_element_type=jnp.float32) * HEAD_DIM ** -0.5
        dist = (q_pos[:, None] - k_pos[None, :]).astype(jnp.float32)
        s = s - slopes[:, None, None, None] * dist
        scores.append(jnp.where(k_pos[None, :] <= q_pos[:, None], s, NEG_INF))
    p = jax.nn.softmax(jnp.concatenate(scores, -1), axis=-1)
    w = p[:, :, 0] - lam * p[:, :, 1]
    return weight_values(w, [v for _, v, _ in segs])


def fox_block(q, cq, q_pos, segs):
    scores = []
    for k, _, ck, k_pos in segs:
        s = jnp.einsum('bqhd,bkhd->bhqk', q, k, preferred_element_type=jnp.float32) * HEAD_DIM ** -0.5
        s = s + jnp.swapaxes(cq, 1, 2)[..., :, None] - jnp.swapaxes(ck, 1, 2)[..., None, :]
        scores.append(jnp.where(k_pos[None, :] <= q_pos[:, None], s, NEG_INF))
    p = jax.nn.softmax(jnp.concatenate(scores, -1), axis=-1)
    return weight_values(p, [v for _, v, _, _ in segs])


def retention(q, k, v, s0, log_gamma):
    b, L, H, Dh = q.shape
    C = L if L % RET_CHUNK else RET_CHUNK
    n = L // C
    f32 = jnp.float32
    qc, kc, vc = (t.astype(f32).reshape(b, n, C, H, Dh) for t in (q, k, v))
    idx = jnp.arange(C, dtype=f32)
    rel = idx[:, None] - idx[None, :]
    decay = jnp.where(rel >= 0, jnp.exp(log_gamma[:, None, None] * jnp.maximum(rel, 0.0)), 0.0)
    o_intra = jnp.einsum('bnhij,bnjhe->bnihe', jnp.einsum('bnihd,bnjhd->bnhij', qc, kc) * decay, vc)
    k_w = jnp.exp(log_gamma[:, None] * (C - 1 - idx)[None, :])
    q_w = jnp.exp(log_gamma[:, None] * (idx + 1)[None, :])
    chunk_kv = jnp.einsum('bnjhd,hj,bnjhe->nbhde', kc, k_w, vc)
    g_chunk = jnp.exp(log_gamma * C)[None, :, None, None]

    def step(s, kv):
        return g_chunk * s + kv, s

    s_last, s_start = lax.scan(step, s0.astype(f32), chunk_kv)
    o_cross = jnp.einsum('bnihd,hi,nbhde->bnihe', qc, q_w, s_start)
    return (o_intra + o_cross).reshape(b, L, H, Dh), s_last


def mem_attn(q, mk, mv):
    s = jnp.einsum('bqhd,bmhd->bhqm', q, mk, preferred_element_type=jnp.float32) * HEAD_DIM ** -0.5
    p = jax.nn.softmax(s, axis=-1)
    return jnp.einsum('bhqm,bmhd->bqhd', p.astype(mv.dtype), mv)


def trunk_layer(x, past, ret_s0, mem_k, mem_v, w_in, b_f, lam_q1, lam_k1, lam_q2, lam_k2,
                a_subln_g, w_out, ln_g, ln_b, layer_idx):
    b, L, _ = x.shape
    f32 = jnp.float32
    n_past = 0 if past is None else past[0].shape[1]
    q_pos = n_past + jnp.arange(L, dtype=jnp.int32)
    past_pos = jnp.arange(n_past, dtype=jnp.int32)
    proj = jnp.einsum('bld,dp->blp', x, w_in)
    (a_q, a_k, a_v, a_g, b_q, b_k, b_v, b_g,
     c_q, c_k, c_v, c_g, c_f, m_q, m_g) = jnp.split(proj, split_points(), axis=-1)

    a_q = a_q.reshape(b, L, H_A, 2, HEAD_DIM)
    a_k = a_k.reshape(b, L, H_A, 2 * HEAD_DIM)
    a_v = a_v.reshape(b, L, H_A, 2 * HEAD_DIM)
    segs_a = [(a_k.reshape(b, L, H_A, 2, HEAD_DIM), a_v, q_pos)]
    if past is not None:
        segs_a = [(past[0].reshape(b, n_past, H_A, 2, HEAD_DIM), past[1], past_pos)] + segs_a
    lam_init = 0.8 - 0.6 * math.exp(-0.3 * layer_idx)
    lam = (jnp.exp(jnp.sum(lam_q1.astype(f32) * lam_k1.astype(f32)))
           - jnp.exp(jnp.sum(lam_q2.astype(f32) * lam_k2.astype(f32))) + lam_init)
    slopes = alibi_slopes()
    o_a = query_sweep(lambda q, qp: diff_attn_block(q, qp, segs_a, lam, slopes), (a_q,), q_pos)
    o_a = rms_norm(o_a, a_subln_g) * (1.0 - lam_init)

    o_b, s_new = retention(b_q.reshape(b, L, H_B, HEAD_DIM),
                           b_k.reshape(b, L, H_B, HEAD_DIM) * HEAD_DIM ** -0.5,
                           b_v.reshape(b, L, H_B, HEAD_DIM), ret_s0, retention_log_decay())
    o_b = rms_norm(o_b).astype(x.dtype)

    c_q = c_q.reshape(b, L, H_C, HEAD_DIM)
    c_k = c_k.reshape(b, L, H_C, HEAD_DIM)
    c_v = c_v.reshape(b, L, H_C, HEAD_DIM)
    logf = jax.nn.log_sigmoid((c_f + b_f).astype(f32))
    if past is None:
        cum = jnp.cumsum(logf, axis=1)
        segs_c = [(c_k, c_v, cum, q_pos)]
    else:
        cum = jnp.cumsum(jnp.concatenate([past[4].astype(f32), logf], axis=1), axis=1)
        segs_c = [(past[2], past[3], cum[:, :n_past], past_pos), (c_k, c_v, cum[:, n_past:], q_pos)]
    o_c = query_sweep(lambda q, cq, qp: fox_block(q, cq, qp, segs_c), (c_q, cum[:, n_past:]), q_pos)

    o_m = mem_attn(m_q.reshape(b, L, H_M, HEAD_DIM), mem_k, mem_v)

    o = jnp.concatenate([o_a.reshape(b, L, W_A) * jax.nn.silu(a_g),
                         o_b.reshape(b, L, W_B) * jax.nn.silu(b_g),
                         o_c.reshape(b, L, W_C) * jax.nn.silu(c_g),
                         o_m.reshape(b, L, W_M) * jax.nn.silu(m_g)], axis=-1)
    y = jnp.einsum('blm,md->bld', o, w_out)
    x_new = layer_norm(ALPHA * x + y, ln_g, ln_b)
    return x_new, (a_k, a_v, c_k, c_v, logf.astype(x.dtype), s_new.astype(x.dtype))


def setup_inputs(seed: int = 0) -> dict:
    key = jax.random.key(seed)
    ks = jax.random.split(key, 24)
    f32 = jnp.float32
    n_pages = PAST_LEN // PAGE_SIZE
    n_used = DEC_BATCH * n_pages
    n_phys = n_used + n_used // 4

    def nrm(k, shape, s):
        return jax.random.normal(k, shape, f32) * s

    col_scale = jnp.concatenate([jnp.full((w,), BETA if name.endswith('_v') else 1.0, f32)
                                 for name, w in IN_GROUPS])
    mem_scale = jnp.concatenate([jnp.ones((W_M,), f32), jnp.full((W_M,), BETA, f32)])
    page_table = jax.random.permutation(ks[10], n_phys)[:n_used].reshape(DEC_BATCH, n_pages).astype(jnp.int32)
    return {
        'x_prompt': nrm(ks[0], (BATCH, SEQ, D_MODEL), 1.0),
        'x_sample': nrm(ks[1], (DEC_BATCH, DEC_SEQ, D_MODEL), 1.0),
        'cache_a_k': nrm(ks[2], (DEPTH, n_phys, PAGE_SIZE, H_A, 2 * HEAD_DIM), 1.0),
        'cache_a_v': nrm(ks[3], (DEPTH, n_phys, PAGE_SIZE, H_A, 2 * HEAD_DIM), BETA),
        'cache_c_k': nrm(ks[4], (DEPTH, n_phys, PAGE_SIZE, H_C, HEAD_DIM), 1.0),
        'cache_c_v': nrm(ks[5], (DEPTH, n_phys, PAGE_SIZE, H_C, HEAD_DIM), BETA),
        'cache_c_logf': jax.nn.log_sigmoid(FORGET_BIAS + nrm(ks[6], (DEPTH, n_phys, PAGE_SIZE, H_C), 1.0)),
        'cache_mem_k': nrm(ks[7], (DEPTH, DEC_BATCH, N_MEM, H_M, HEAD_DIM), 1.0),
        'cache_mem_v': nrm(ks[8], (DEPTH, DEC_BATCH, N_MEM, H_M, HEAD_DIM), BETA),
        'state_ret': nrm(ks[9], (DEPTH, DEC_BATCH, H_B, HEAD_DIM, HEAD_DIM), 0.5),
        'page_table': page_table,
        'mem_prompt': nrm(ks[11], (BATCH, N_MEM, D_MODEL), 1.0),
        'w_in': nrm(ks[12], (DEPTH, D_MODEL, P_IN), D_MODEL ** -0.5) * col_scale,
        'b_f': FORGET_BIAS + nrm(ks[13], (DEPTH, H_C), 0.1),
        'lam_q1': nrm(ks[14], (DEPTH, HEAD_DIM), 0.1),
        'lam_k1': nrm(ks[15], (DEPTH, HEAD_DIM), 0.1),
        'lam_q2': nrm(ks[16], (DEPTH, HEAD_DIM), 0.1),
        'lam_k2': nrm(ks[17], (DEPTH, HEAD_DIM), 0.1),
        'a_subln_g': 1.0 + nrm(ks[18], (DEPTH, 2 * HEAD_DIM), 0.02),
        'w_out': nrm(ks[19], (DEPTH, MIX, D_MODEL), MIX ** -0.5 * BETA),
        'w_mem_kv': nrm(ks[20], (DEPTH, D_MODEL, 2 * W_M), D_MODEL ** -0.5) * mem_scale,
        'ln_g': 1.0 + nrm(ks[21], (DEPTH, D_MODEL), 0.02),
        'ln_b': nrm(ks[22], (DEPTH, D_MODEL), 0.02),
    }


def reference(x_prompt, x_sample, cache_a_k, cache_a_v, cache_c_k, cache_c_v, cache_c_logf,
              cache_mem_k, cache_mem_v, state_ret, page_table, mem_prompt, w_in, b_f,
              lam_q1, lam_k1, lam_q2, lam_k2, a_subln_g, w_out, w_mem_kv, ln_g, ln_b):
    def gather_pages(pool, l):
        g = pool[l, page_table]
        return g.reshape(g.shape[0], g.shape[1] * g.shape[2], *g.shape[3:])

    xp, xs = x_prompt, x_sample
    bp = x_prompt.shape[0]
    new_p, new_s, mem_k_p, mem_v_p = [], [], [], []
    for l in range(DEPTH):
        weights = (w_in[l], b_f[l], lam_q1[l], lam_k1[l], lam_q2[l], lam_k2[l],
                   a_subln_g[l], w_out[l], ln_g[l], ln_b[l])
        mkv = jnp.einsum('bmd,de->bme', mem_prompt, w_mem_kv[l])
        mk = mkv[..., :W_M].reshape(bp, N_MEM, H_M, HEAD_DIM)
        mv = mkv[..., W_M:].reshape(bp, N_MEM, H_M, HEAD_DIM)
        s0 = jnp.zeros((bp, H_B, HEAD_DIM, HEAD_DIM), jnp.float32)
        xp, st = trunk_layer(xp, None, s0, mk, mv, *weights, l)
        new_p.append(st)
        mem_k_p.append(mk)
        mem_v_p.append(mv)
        past = tuple(gather_pages(c, l) for c in (cache_a_k, cache_a_v, cache_c_k, cache_c_v, cache_c_logf))
        xs, st = trunk_layer(xs, past, state_ret[l], cache_mem_k[l], cache_mem_v[l], *weights, l)
        new_s.append(st)

    def stk(sts, i):
        return jnp.stack([st[i] for st in sts])

    return (xp, xs,
            stk(new_p, 0), stk(new_p, 1), stk(new_p, 2), stk(new_p, 3), stk(new_p, 4), stk(new_p, 5),
            jnp.stack(mem_k_p), jnp.stack(mem_v_p),
            stk(new_s, 0), stk(new_s, 1), stk(new_s, 2), stk(new_s, 3), stk(new_s, 4), stk(new_s, 5))
```

```python
import functools
import math

import jax
import jax.numpy as jnp
import numpy as np
from jax import lax
from jax.experimental import pallas as pl
from jax.experimental.pallas import tpu as pltpu

F32 = jnp.float32
BF16 = jnp.bfloat16

HEAD_DIM = 64
H_A, H_B, H_C, H_M = 4, 6, 6, 4
W_A, W_B, W_C, W_M = H_A * 2 * HEAD_DIM, H_B * HEAD_DIM, H_C * HEAD_DIM, H_M * HEAD_DIM
MIX = W_A + W_B + W_C + W_M
LANES = 128
SUBLANES = 8
RET_CHUNK = 128
LN_EPS = 1e-5
NEG_INF = -1e30
QK_SCALE = HEAD_DIM ** -0.5
VMEM_LIMIT = 48 * 1024 * 1024

_REF_GROUPS = (('a_q', W_A), ('a_k', W_A), ('a_v', W_A), ('a_g', W_A),
               ('b_q', W_B), ('b_k', W_B), ('b_v', W_B), ('b_g', W_B),
               ('c_q', W_C), ('c_k', W_C), ('c_v', W_C), ('c_g', W_C), ('c_f', H_C),
               ('m_q', W_M), ('m_g', W_M))
_GROUPS = (('c_q', W_C), ('c_k', W_C), ('c_v', W_C), ('c_g', W_C),
           ('b_q', W_B), ('b_k', W_B), ('b_v', W_B), ('b_g', W_B),
           ('a_q', W_A), ('a_k', W_A), ('a_v', W_A), ('a_g', W_A),
           ('m_q', W_M), ('m_g', W_M), ('c_f', LANES))
REF_COL, COL = {}, {}
_o = 0
for _n, _w in _REF_GROUPS:
    REF_COL[_n] = (_o, _w)
    _o += _w
_o = 0
for _n, _w in _GROUPS:
    assert _o % _w == 0 and _o % LANES == 0
    COL[_n] = _o
    _o += _w
PW = _o
BLK = {n: o // LANES for n, o in COL.items()}


def _permute_w_in(w_in):
    parts = [w_in[:, :, REF_COL[n][0]:REF_COL[n][0] + REF_COL[n][1]] for n, _ in _GROUPS]
    parts.append(jnp.zeros(w_in.shape[:2] + (LANES - H_C,), w_in.dtype))
    return jnp.concatenate(parts, axis=-1)


def _cparams(sem):
    return pltpu.CompilerParams(dimension_semantics=sem, vmem_limit_bytes=VMEM_LIMIT)


def _silu(x):
    return x / (1.0 + jnp.exp(-x))


def _dot(a, b):
    return jnp.dot(a.astype(BF16), b.astype(BF16), preferred_element_type=F32)


def _dot_nt(a, b):
    return lax.dot_general(a.astype(BF16), b.astype(BF16), (((1,), (1,)), ((), ())),
                           preferred_element_type=F32)


def _lane_half_masks(shape):
    lane = lax.broadcasted_iota(jnp.int32, shape, len(shape) - 1)
    lo = (lane % LANES) < HEAD_DIM
    return lo, jnp.logical_not(lo)


def _proj_kernel(x_ref, w_ref, o_ref, xb_ref):
    @pl.when(pl.program_id(1) == 0)
    def _():
        xb_ref[...] = x_ref[...].astype(BF16)

    o_ref[...] = jnp.dot(xb_ref[...], w_ref[...], preferred_element_type=F32)


def _proj(x2d, w_all, layer, tm, tn):
    m, k = x2d.shape
    n = w_all.shape[2]
    return pl.pallas_call(
        _proj_kernel,
        out_shape=jax.ShapeDtypeStruct((m, n), F32),
        grid=(m // tm, n // tn),
        in_specs=[pl.BlockSpec((tm, k), lambda i, j: (i, 0)),
                  pl.BlockSpec((None, k, tn), lambda i, j: (layer, 0, j))],
        out_specs=pl.BlockSpec((tm, tn), lambda i, j: (i, j)),
        scratch_shapes=[pltpu.VMEM((tm, k), BF16)],
        compiler_params=_cparams(("parallel", "arbitrary")),
        name="proj",
    )(x2d, w_all)


def _log_sigmoid(z):
    return jnp.minimum(z, 0.0) - jnp.log1p(jnp.exp(-jnp.abs(z)))


def _lane_cumsum(x):
    n = x.shape[-1]
    lane = lax.broadcasted_iota(jnp.int32, x.shape, x.ndim - 1)
    shift = 1
    while shift < n:
        x = x + jnp.where(lane >= shift, pltpu.roll(x, shift, x.ndim - 1), 0.0)
        shift *= 2
    return x


def _logf_cum_kernel(cf_ref, bf_ref, logf_ref, cum_ref):
    lf = _log_sigmoid(cf_ref[...] + bf_ref[...])
    lf_t = lf.T[:SUBLANES, :]
    logf_ref[...] = lf_t
    cum_ref[...] = _lane_cumsum(lf_t)


def _logf_cum(proj3, bf_pad):
    b, seq, _ = proj3.shape
    out = jax.ShapeDtypeStruct((b, SUBLANES, seq), F32)
    return pl.pallas_call(
        _logf_cum_kernel,
        out_shape=(out, out),
        grid=(b,),
        in_specs=[pl.BlockSpec((None, seq, LANES), lambda i: (i, 0, BLK['c_f'])),
                  pl.BlockSpec((1, LANES), lambda i: (0, 0))],
        out_specs=(pl.BlockSpec((None, SUBLANES, seq), lambda i: (i, 0, 0)),
                   pl.BlockSpec((None, SUBLANES, seq), lambda i: (i, 0, 0))),
        compiler_params=_cparams(("parallel",)),
        name="logf_cum",
    )(proj3, bf_pad)


def _tri_pairs(n):
    qi = np.array([i for i in range(n) for _ in range(i + 1)], np.int32)
    kj = np.array([j for i in range(n) for j in range(i + 1)], np.int32)
    return jnp.asarray(qi), jnp.asarray(kj)


def _stack_queries(q_ref, q2_ref, tq):
    q = q_ref[...] * QK_SCALE
    lo, hi = _lane_half_masks(q.shape)
    q2_ref[0:tq, :] = jnp.where(lo, q, 0.0).astype(BF16)
    q2_ref[tq:, :] = jnp.where(hi, q, 0.0).astype(BF16)


def _flash_update(s, v_ref, m_ref, l_ref, acc_ref):
    m_old = m_ref[...]
    m_new = jnp.maximum(m_old, jnp.max(s, axis=1, keepdims=True))
    alpha = jnp.exp(m_old - m_new)
    p = jnp.exp(s - m_new)
    l_ref[...] = alpha * l_ref[...] + jnp.sum(p, axis=1, keepdims=True)
    acc_ref[...] = alpha * acc_ref[...] + _dot(p, v_ref[...])
    m_ref[...] = m_new


def _flash_init(m_ref, l_ref, acc_ref):
    m_ref[...] = jnp.full(m_ref.shape, NEG_INF, F32)
    l_ref[...] = jnp.zeros(l_ref.shape, F32)
    acc_ref[...] = jnp.zeros(acc_ref.shape, F32)


def _rel_pos(shape, tq, i, j):
    r = lax.broadcasted_iota(jnp.int32, shape, 0)
    r = jnp.where(r >= tq, r - tq, r)
    c = lax.broadcasted_iota(jnp.int32, shape, 1)
    return (c - r) + (j - i) * tq


def _lambda(lamv_ref, lam_init):
    lv = lamv_ref[...]
    e1 = jnp.exp(jnp.sum(lv[0:1] * lv[1:2], axis=1, keepdims=True))
    e2 = jnp.exp(jnp.sum(lv[2:3] * lv[3:4], axis=1, keepdims=True))
    return e1 - e2 + lam_init


def _attn_a_kernel(qi_ref, kj_ref, slopes_ref, lamv_ref, g_ref, q_ref, k_ref, v_ref, gate_ref,
                   o_ref, q2_ref, m_ref, l_ref, acc_ref, *, tq, lam_init):
    h = pl.program_id(1)
    p = pl.program_id(2)
    i = qi_ref[p]
    j = kj_ref[p]
    slope = slopes_ref[h]

    @pl.when(j == 0)
    def _():
        _stack_queries(q_ref, q2_ref, tq)
        _flash_init(m_ref, l_ref, acc_ref)

    def scores():
        s = _dot_nt(q2_ref[...], k_ref[...])
        d = _rel_pos(s.shape, tq, i, j)
        return s + slope * d.astype(F32), d

    @pl.when(j < i)
    def _():
        s, _ = scores()
        _flash_update(s, v_ref, m_ref, l_ref, acc_ref)

    @pl.when(j == i)
    def _():
        s, d = scores()
        _flash_update(jnp.where(d <= 0, s, NEG_INF), v_ref, m_ref, l_ref, acc_ref)
        o = acc_ref[...] / l_ref[...]
        o = o[:tq] - _lambda(lamv_ref, lam_init) * o[tq:]
        y = o * lax.rsqrt(jnp.mean(o * o, axis=1, keepdims=True) + LN_EPS) * g_ref[...]
        o_ref[...] = y * (1.0 - lam_init) * _silu(gate_ref[...])


def _attn_a(proj3, slopes, lamv, g, lam_init, tq):
    b, seq, _ = proj3.shape
    qi, kj = _tri_pairs(seq // tq)

    def blk(col, tile):
        return pl.BlockSpec((None, tq, LANES),
                            lambda bb, h, p, qi, kj: (bb, (qi if tile == 'q' else kj)[p], col + h))

    kern = functools.partial(_attn_a_kernel, tq=tq, lam_init=lam_init)
    return pl.pallas_call(
        kern,
        out_shape=jax.ShapeDtypeStruct((b, seq, W_A), F32),
        grid_spec=pltpu.PrefetchScalarGridSpec(
            num_scalar_prefetch=2,
            grid=(b, H_A, qi.shape[0]),
            in_specs=[pl.BlockSpec(memory_space=pltpu.SMEM),
                      pl.BlockSpec((4, HEAD_DIM), lambda bb, h, p, qi, kj: (0, 0)),
                      pl.BlockSpec((1, LANES), lambda bb, h, p, qi, kj: (0, 0)),
                      blk(BLK['a_q'], 'q'), blk(BLK['a_k'], 'k'), blk(BLK['a_v'], 'k'),
                      blk(BLK['a_g'], 'q')],
            out_specs=pl.BlockSpec((None, tq, LANES), lambda bb, h, p, qi, kj: (bb, qi[p], h)),
            scratch_shapes=[pltpu.VMEM((2 * tq, LANES), BF16),
                            pltpu.VMEM((2 * tq, 1), F32),
                            pltpu.VMEM((2 * tq, 1), F32),
                            pltpu.VMEM((2 * tq, LANES), F32)]),
        compiler_params=_cparams(("parallel", "parallel", "arbitrary")),
        name="attn_a",
    )(qi, kj, slopes, lamv, g, proj3, proj3, proj3, proj3)


def _attn_c_kernel(qi_ref, kj_ref, q_ref, k_ref, v_ref, cum_ref, gate_ref,
                   o_ref, q2_ref, m_ref, l_ref, acc_ref, *, tq):
    hb = pl.program_id(1)
    p = pl.program_id(2)
    i = qi_ref[p]
    j = kj_ref[p]

    @pl.when(j == 0)
    def _():
        _stack_queries(q_ref, q2_ref, tq)
        _flash_init(m_ref, l_ref, acc_ref)

    def scores():
        s = _dot_nt(q2_ref[...], k_ref[...])
        ck_lo = cum_ref[pl.ds(2 * hb, 1), :]
        ck_hi = cum_ref[pl.ds(2 * hb + 1, 1), :]
        return jnp.concatenate([s[:tq] - ck_lo, s[tq:] - ck_hi], axis=0)

    @pl.when(j < i)
    def _():
        _flash_update(scores(), v_ref, m_ref, l_ref, acc_ref)

    @pl.when(j == i)
    def _():
        s = scores()
        d = _rel_pos(s.shape, tq, i, j)
        _flash_update(jnp.where(d <= 0, s, NEG_INF), v_ref, m_ref, l_ref, acc_ref)
        o = acc_ref[...] / l_ref[...]
        lo, _ = _lane_half_masks((tq, LANES))
        o_ref[...] = jnp.where(lo, o[:tq], o[tq:]) * _silu(gate_ref[...])


def _attn_c(proj3, cum_t, tq):
    b, seq, _ = proj3.shape
    qi, kj = _tri_pairs(seq // tq)

    def blk(col, tile):
        return pl.BlockSpec((None, tq, LANES),
                            lambda bb, h, p, qi, kj: (bb, (qi if tile == 'q' else kj)[p], col + h))

    return pl.pallas_call(
        functools.partial(_attn_c_kernel, tq=tq),
        out_shape=jax.ShapeDtypeStruct((b, seq, W_C), F32),
        grid_spec=pltpu.PrefetchScalarGridSpec(
            num_scalar_prefetch=2,
            grid=(b, H_C // 2, qi.shape[0]),
            in_specs=[blk(BLK['c_q'], 'q'), blk(BLK['c_k'], 'k'), blk(BLK['c_v'], 'k'),
                      pl.BlockSpec((None, SUBLANES, tq), lambda bb, h, p, qi, kj: (bb, 0, kj[p])),
                      blk(BLK['c_g'], 'q')],
            out_specs=pl.BlockSpec((None, tq, LANES), lambda bb, h, p, qi, kj: (bb, qi[p], h)),
            scratch_shapes=[pltpu.VMEM((2 * tq, LANES), BF16),
                            pltpu.VMEM((2 * tq, 1), F32),
                            pltpu.VMEM((2 * tq, 1), F32),
                            pltpu.VMEM((2 * tq, LANES), F32)]),
        compiler_params=_cparams(("parallel", "parallel", "arbitrary")),
        name="attn_c",
    )(qi, kj, proj3, proj3, proj3, cum_t, proj3)


def _ret_tables(chunk, n_valid):
    log_gamma = jnp.log(1.0 - 2.0 ** (-5.0 - jnp.arange(H_B, dtype=F32)))
    idx = jnp.arange(chunk, dtype=F32)
    valid = idx < n_valid
    rel = idx[:, None] - idx[None, :]
    decay = jnp.where(rel >= 0, jnp.exp(log_gamma[:, None, None] * jnp.maximum(rel, 0.0)), 0.0)
    key_rows = max(chunk, LANES)
    decay = jnp.pad(decay, ((0, 0), (0, 0), (0, key_rows - chunk)))
    k_w = jnp.where(valid, jnp.exp(log_gamma[:, None] * (n_valid - 1 - idx)[None, :]), 0.0)
    q_w = jnp.exp(log_gamma[:, None] * (idx + 1)[None, :])
    g = jnp.exp(log_gamma * n_valid)

    def lanes(t):
        t = t.reshape(H_B // 2, 2, chunk)
        return jnp.repeat(jnp.transpose(t, (0, 2, 1)), HEAD_DIM, axis=2)

    g_rows = jnp.repeat(g.reshape(H_B // 2, 2), HEAD_DIM, axis=1)
    g_tab = jnp.broadcast_to(g_rows[:, :, None], (H_B // 2, LANES, LANES))
    return decay.reshape(H_B // 2, 2, chunk, key_rows), lanes(k_w) * QK_SCALE, lanes(q_w), g_tab


def _retention_kernel(q_ref, k_ref, v_ref, gate_ref, s0_ref, decay_ref, kw_ref, qw_ref, g_ref,
                      o_ref, s_out_ref, s_ref, *, chunk, n_chunks):
    t = pl.program_id(2)

    @pl.when(t == 0)
    def _():
        s_ref[...] = s0_ref[...]

    shape = (chunk, LANES)
    lo, hi = _lane_half_masks(shape)
    r = lax.broadcasted_iota(jnp.int32, (LANES, LANES), 0)
    c = lax.broadcasted_iota(jnp.int32, (LANES, LANES), 1)
    same_head = (r < HEAD_DIM) == (c < HEAD_DIM)

    def body(n, carry):
        rows = pl.ds(pl.multiple_of(n * chunk, chunk), chunk)
        q = q_ref[rows, :]
        k = k_ref[rows, :]
        v = v_ref[rows, :]
        kw = k * kw_ref[...]
        if chunk < LANES:
            pad = jnp.zeros((LANES - chunk, LANES), F32)
            k, v, kw = (jnp.concatenate([a, pad], axis=0) for a in (k, v, kw))
        lo_k, hi_k = _lane_half_masks(v.shape)
        state = s_ref[...]
        o_cross = _dot(q * qw_ref[...], state)
        a_lo = _dot_nt(jnp.where(lo, q, 0.0), k) * (QK_SCALE * decay_ref[0])
        a_hi = _dot_nt(jnp.where(hi, q, 0.0), k) * (QK_SCALE * decay_ref[1])
        o = o_cross + _dot(a_lo, jnp.where(lo_k, v, 0.0)) + _dot(a_hi, jnp.where(hi_k, v, 0.0))
        kv = _dot(kw.T, v)
        s_ref[...] = g_ref[...] * state + jnp.where(same_head, kv, 0.0)
        sq = o * o
        ms_lo = jnp.sum(jnp.where(lo, sq, 0.0), axis=1, keepdims=True) * (1.0 / HEAD_DIM)
        ms_hi = jnp.sum(jnp.where(hi, sq, 0.0), axis=1, keepdims=True) * (1.0 / HEAD_DIM)
        y = o * lax.rsqrt(jnp.where(lo, ms_lo, ms_hi) + LN_EPS)
        o_ref[rows, :] = y * _silu(gate_ref[rows, :])
        return carry

    lax.fori_loop(0, n_chunks, body, 0)

    @pl.when(t == pl.num_programs(2) - 1)
    def _():
        s_out_ref[...] = s_ref[...]


def _retention(proj3, s0_bd, tables, chunk, rows_per_step):
    b, seq, _ = proj3.shape
    decay, kw, qw, g_tab = tables
    n_pairs = H_B // 2
    n_chunks = rows_per_step // chunk

    def blk(col):
        return pl.BlockSpec((None, rows_per_step, LANES), lambda bb, h, t: (bb, t, col + h))

    def tab(shape):
        return pl.BlockSpec((None,) + shape, lambda bb, h, t: (h,) + (0,) * len(shape))

    return pl.pallas_call(
        functools.partial(_retention_kernel, chunk=chunk, n_chunks=n_chunks),
        out_shape=(jax.ShapeDtypeStruct((b, seq, W_B), F32),
                   jax.ShapeDtypeStruct((b, n_pairs, LANES, LANES), F32)),
        grid=(b, n_pairs, seq // rows_per_step),
        in_specs=[blk(BLK['b_q']), blk(BLK['b_k']), blk(BLK['b_v']), blk(BLK['b_g']),
                  pl.BlockSpec((None, None, LANES, LANES), lambda bb, h, t: (bb, h, 0, 0)),
                  tab((2, chunk, max(chunk, LANES))), tab((chunk, LANES)), tab((chunk, LANES)),
                  tab((LANES, LANES))],
        out_specs=(pl.BlockSpec((None, rows_per_step, LANES), lambda bb, h, t: (bb, t, h)),
                   pl.BlockSpec((None, None, LANES, LANES), lambda bb, h, t: (bb, h, 0, 0))),
        scratch_shapes=[pltpu.VMEM((LANES, LANES), F32)],
        compiler_params=_cparams(("parallel", "parallel", "arbitrary")),
        name="retention",
    )(proj3, proj3, proj3, proj3, s0_bd, decay, kw, qw, g_tab)


def _state_to_blockdiag(s):
    b = s.shape[0]
    s = s.reshape(b, H_B // 2, 2, HEAD_DIM, HEAD_DIM)
    z = jnp.zeros_like(s[:, :, 0])
    top = jnp.concatenate([s[:, :, 0], z], axis=-1)
    bot = jnp.concatenate([z, s[:, :, 1]], axis=-1)
    return jnp.concatenate([top, bot], axis=-2)


def _state_from_blockdiag(s):
    b = s.shape[0]
    even = s[:, :, :HEAD_DIM, :HEAD_DIM]
    odd = s[:, :, HEAD_DIM:, HEAD_DIM:]
    return jnp.stack([even, odd], axis=2).reshape(b, H_B, HEAD_DIM, HEAD_DIM)


def _mem_attn_kernel(q_ref, gate_ref, mk_ref, mv_ref, o_ref):
    q = q_ref[...] * QK_SCALE
    lo, hi = _lane_half_masks((q.shape[0], LANES))
    outs = []
    for pair in range(H_M // 2):
        cols = slice(pair * LANES, (pair + 1) * LANES)
        qp = q[:, cols]
        mk = mk_ref[:, cols]
        mv = mv_ref[:, cols]
        o_pair = None
        for mask in (lo, hi):
            s = _dot_nt(jnp.where(mask, qp, 0.0), mk)
            p = jnp.exp(s - jnp.max(s, axis=1, keepdims=True))
            o = _dot(p, mv) / jnp.sum(p, axis=1, keepdims=True)
            o_pair = jnp.where(mask, o, 0.0) if o_pair is None else jnp.where(mask, o, o_pair)
        outs.append(o_pair)
    o_ref[...] = jnp.concatenate(outs, axis=1) * _silu(gate_ref[...])


def _mem_attn_t_kernel(q_ref, gate_ref, mkt_ref, mvt_ref, o_ref):
    q = q_ref[...] * QK_SCALE
    col = lax.broadcasted_iota(jnp.int32, q.shape, 1)
    o_all = None
    for h in range(H_M):
        head = col // HEAD_DIM == h
        s = _dot(jnp.where(head, q, 0.0), mkt_ref[...])
        p = jnp.exp(s - jnp.max(s, axis=1, keepdims=True))
        o = _dot_nt(p, mvt_ref[...]) / jnp.sum(p, axis=1, keepdims=True)
        o_all = jnp.where(head, o, 0.0) if o_all is None else jnp.where(head, o, o_all)
    o_ref[...] = o_all * _silu(gate_ref[...])


def _mem_attn(proj3, mk_arr, mv_arr, mk_blk, mv_blk, tq, transposed, mem_off=0):
    b, seq, _ = proj3.shape
    mem_block = (None, W_M, mk_arr.shape[2]) if transposed else (None, mk_arr.shape[1], W_M)
    return pl.pallas_call(
        _mem_attn_t_kernel if transposed else _mem_attn_kernel,
        out_shape=jax.ShapeDtypeStruct((b, seq, W_M), F32),
        grid=(b, seq // tq),
        in_specs=[pl.BlockSpec((None, tq, W_M), lambda bb, i: (bb, i, COL['m_q'] // W_M)),
                  pl.BlockSpec((None, tq, W_M), lambda bb, i: (bb, i, COL['m_g'] // W_M)),
                  pl.BlockSpec(mem_block, lambda bb, i: (bb + mem_off, 0, mk_blk)),
                  pl.BlockSpec(mem_block, lambda bb, i: (bb + mem_off, 0, mv_blk))],
        out_specs=pl.BlockSpec((None, tq, W_M), lambda bb, i: (bb, i, 0)),
        compiler_params=_cparams(("parallel", "parallel")),
        name="mem_attn",
    )(proj3, proj3, mk_arr, mv_arr)


def _out_ln_kernel(oa_ref, ob_ref, oc_ref, om_ref, w_ref, x_ref, g_ref, b_ref, y_ref, *, alpha):
    y = _dot(oa_ref[...], w_ref[0:W_A, :])
    y += _dot(ob_ref[...], w_ref[W_A:W_A + W_B, :])
    y += _dot(oc_ref[...], w_ref[W_A + W_B:W_A + W_B + W_C, :])
    y += _dot(om_ref[...], w_ref[W_A + W_B + W_C:, :])
    z = alpha * x_ref[...] + y
    mu = jnp.mean(z, axis=1, keepdims=True)
    zc = z - mu
    var = jnp.mean(zc * zc, axis=1, keepdims=True)
    y_ref[...] = zc * lax.rsqrt(var + LN_EPS) * g_ref[...] + b_ref[...]


def _out_ln(o_a, o_b, o_c, o_m, w_out_all, layer, x2d, ln_g, ln_b, alpha, tm):
    m, d = x2d.shape

    def rows(w):
        return pl.BlockSpec((tm, w), lambda i: (i, 0))

    return pl.pallas_call(
        functools.partial(_out_ln_kernel, alpha=alpha),
        out_shape=jax.ShapeDtypeStruct((m, d), F32),
        grid=(m // tm,),
        in_specs=[rows(W_A), rows(W_B), rows(W_C), rows(W_M),
                  pl.BlockSpec((None, MIX, d), lambda i: (layer, 0, 0)),
                  rows(d),
                  pl.BlockSpec((None, 1, d), lambda i: (layer, 0, 0)),
                  pl.BlockSpec((None, 1, d), lambda i: (layer, 0, 0))],
        out_specs=rows(d),
        compiler_params=_cparams(("parallel",)),
        name="out_ln",
    )(o_a, o_b, o_c, o_m, w_out_all, x2d, ln_g, ln_b)


PAGES_PER_STEP = 8
ROWS_S = 8


def _online_merge(s, m_ref, l_ref):
    m_old = m_ref[...]
    m_new = jnp.maximum(m_old, jnp.max(s, axis=1, keepdims=True))
    alpha = jnp.exp(m_old - m_new)
    p = jnp.exp(s - m_new)
    l_ref[...] = alpha * l_ref[...] + jnp.sum(p, axis=1, keepdims=True)
    m_ref[...] = m_new
    return alpha, p


def _rows_from_tokens(x, n_new):
    return jnp.concatenate([jnp.broadcast_to(x[t:t + 1], (SUBLANES, x.shape[1])) for t in range(n_new)],
                           axis=0)


def _slot_sum(x, n_new):
    return jnp.sum(x.reshape(n_new, SUBLANES, x.shape[1]), axis=1)


def _dec_a_kernel(pt_ref, slopes_ref, lamv_ref, g_ref, q_ref, kn_ref, vn_ref, gate_ref, *rest,
                  n_new, n_past, page, lam_init):
    k_pages = rest[:PAGES_PER_STEP]
    v_pages = rest[PAGES_PER_STEP:2 * PAGES_PER_STEP]
    o_ref, q_rows_ref, m_ref, l_ref, acc_ref = rest[2 * PAGES_PER_STEP:]
    g = pl.program_id(1)
    n_rows = n_new * SUBLANES
    row = lax.broadcasted_iota(jnp.int32, (n_rows, 1), 0)
    t_row = row // SUBLANES
    h_row = (row % SUBLANES) // 2
    slope_row = jnp.zeros((n_rows, 1), F32)
    for h in range(H_A):
        slope_row = jnp.where(h_row == h, slopes_ref[h], slope_row)

    def head_rows(x):
        out = jnp.zeros((n_rows, LANES), F32)
        for h in range(H_A):
            out = jnp.where(h_row == h, _rows_from_tokens(x[:, h * LANES:(h + 1) * LANES], n_new), out)
        return out

    @pl.when(g == 0)
    def _():
        q = head_rows(q_ref[...] * QK_SCALE)
        lane = lax.broadcasted_iota(jnp.int32, q.shape, 1)
        q_rows_ref[...] = jnp.where(lane // HEAD_DIM == row % 2, q, 0.0)
        _flash_init(m_ref, l_ref, acc_ref)

    q_rows = q_rows_ref[...]
    col = lax.broadcasted_iota(jnp.int32, (n_rows, page * H_A), 1)
    own = (col % H_A) == h_row
    rel = (col // H_A - (n_past + t_row)).astype(F32)
    bias0 = jnp.where(own, slope_row * rel, NEG_INF)
    s_parts = []
    for n, kp in enumerate(k_pages):
        base = ((g * PAGES_PER_STEP + n) * page).astype(F32)
        s_parts.append(_dot_nt(q_rows, kp[...]) + (bias0 + slope_row * base))
    alpha, p = _online_merge(jnp.concatenate(s_parts, axis=1), m_ref, l_ref)
    acc = alpha * acc_ref[...]
    w = page * H_A
    for n, vp in enumerate(v_pages):
        acc += _dot(p[:, n * w:(n + 1) * w], vp[...])
    acc_ref[...] = acc

    @pl.when(g == pl.num_programs(1) - 1)
    def _():
        kn = kn_ref[...]
        vn = vn_ref[...]
        s_new = []
        for u in range(n_new):
            ku = head_rows(jnp.broadcast_to(kn[u:u + 1], (n_new, W_A)))
            su = jnp.sum(q_rows * ku, axis=1, keepdims=True) + slope_row * (u - t_row).astype(F32)
            s_new.append(jnp.where(t_row >= u, su, NEG_INF))
        alpha, p = _online_merge(jnp.concatenate(s_new, axis=1), m_ref, l_ref)
        acc = alpha * acc_ref[...]
        for u in range(n_new):
            acc += p[:, u:u + 1] * head_rows(jnp.broadcast_to(vn[u:u + 1], (n_new, W_A)))
        o = acc / l_ref[...]
        o = o * jnp.where(row % 2 == 0, 1.0, -_lambda(lamv_ref, lam_init))
        outs = []
        for h in range(H_A):
            oh = _slot_sum(jnp.where(h_row == h, o, 0.0), n_new)
            yh = oh * lax.rsqrt(jnp.mean(oh * oh, axis=1, keepdims=True) + LN_EPS) * g_ref[...]
            outs.append(yh * (1.0 - lam_init))
        y = jnp.concatenate(outs, axis=1) * _silu(gate_ref[...][:n_new])
        o_ref[...] = jnp.concatenate([y, jnp.zeros((ROWS_S - n_new, W_A), F32)], axis=0)


def _dec_a(proj3, cache_k, cache_v, layer, pt_flat, slopes, lamv, g, lam_init, n_new):
    b = proj3.shape[0]
    page = cache_k.shape[2] // H_A
    n_pages = pt_flat.shape[0] // b
    n_steps = n_pages // PAGES_PER_STEP

    def new_blk(col):
        return pl.BlockSpec((None, ROWS_S, W_A), lambda bb, s, pt: (bb, 0, col // W_A))

    def page_blk(n):
        return pl.BlockSpec((None, None, page * H_A, LANES),
                            lambda bb, s, pt: (layer, pt[bb * n_pages + s * PAGES_PER_STEP + n], 0, 0))

    n_rows = n_new * SUBLANES
    kern = functools.partial(_dec_a_kernel, n_new=n_new, n_past=n_pages * page, page=page,
                             lam_init=lam_init)
    return pl.pallas_call(
        kern,
        out_shape=jax.ShapeDtypeStruct((b, ROWS_S, W_A), F32),
        grid_spec=pltpu.PrefetchScalarGridSpec(
            num_scalar_prefetch=1,
            grid=(b, n_steps),
            in_specs=[pl.BlockSpec(memory_space=pltpu.SMEM),
                      pl.BlockSpec((4, HEAD_DIM), lambda bb, s, pt: (0, 0)),
                      pl.BlockSpec((1, LANES), lambda bb, s, pt: (0, 0)),
                      new_blk(COL['a_q']), new_blk(COL['a_k']), new_blk(COL['a_v']),
                      new_blk(COL['a_g'])]
                     + [page_blk(n) for n in range(PAGES_PER_STEP)] * 2,
            out_specs=pl.BlockSpec((None, ROWS_S, W_A), lambda bb, s, pt: (bb, 0, 0)),
            scratch_shapes=[pltpu.VMEM((n_rows, LANES), F32),
                            pltpu.VMEM((n_rows, 1), F32),
                            pltpu.VMEM((n_rows, 1), F32),
                            pltpu.VMEM((n_rows, LANES), F32)]),
        compiler_params=_cparams(("parallel", "arbitrary")),
        name="dec_a",
    )(pt_flat, slopes, lamv, g, proj3, proj3, proj3, proj3,
      *([cache_k] * PAGES_PER_STEP), *([cache_v] * PAGES_PER_STEP))


def _dec_c_kernel(pt_ref, bf_ref, q_ref, kn_ref, vn_ref, cf_ref, gate_ref, *rest,
                  n_new, page, n_pages):
    k_pages = rest[:PAGES_PER_STEP]
    v_pages = rest[PAGES_PER_STEP:2 * PAGES_PER_STEP]
    f_pages = rest[2 * PAGES_PER_STEP:3 * PAGES_PER_STEP]
    o_ref, logf_ref, qbd_ref, m_ref, l_ref, acc_ref, run_ref = rest[3 * PAGES_PER_STEP:]
    b = pl.program_id(0)
    g = pl.program_id(1)
    n_rows = n_new * SUBLANES
    row = lax.broadcasted_iota(jnp.int32, (n_rows, 1), 0)
    t_row = row // SUBLANES
    h_row = row % SUBLANES

    @pl.when(g == 0)
    def _():
        qr = _rows_from_tokens(q_ref[...] * QK_SCALE, n_new)
        col = lax.broadcasted_iota(jnp.int32, (n_rows, W_C), 1)
        qbd_ref[...] = jnp.where(col // HEAD_DIM == h_row, qr, 0.0)
        _flash_init(m_ref, l_ref, acc_ref)
        run_ref[...] = jnp.zeros(run_ref.shape, F32)

    qbd = qbd_ref[...]
    run = run_ref[...]
    sub = lax.broadcasted_iota(jnp.int32, (SUBLANES, SUBLANES, page), 1)
    s_parts = []
    for n, (kp, fp) in enumerate(zip(k_pages, f_pages)):
        pid = pt_ref[b * n_pages + g * PAGES_PER_STEP + n]
        lf = jnp.sum(jnp.where(sub == pid % SUBLANES, fp[...], 0.0), axis=1)
        cum = run + _lane_cumsum(lf)
        run = cum[:, page - 1:page]
        s_parts.append(_dot(qbd, kp[...]) - jnp.concatenate([cum] * n_new, axis=0))
    run_ref[...] = run
    alpha, p = _online_merge(jnp.concatenate(s_parts, axis=1), m_ref, l_ref)
    acc = alpha * acc_ref[...]
    for n, vp in enumerate(v_pages):
        acc += _dot_nt(p[:, n * page:(n + 1) * page], vp[...])
    acc_ref[...] = acc

    @pl.when(g == pl.num_programs(1) - 1)
    def _():
        kn = kn_ref[...]
        vn = vn_ref[...]
        lf = _log_sigmoid(cf_ref[...] + bf_ref[...])
        logf_ref[...] = lf
        lane = lax.broadcasted_iota(jnp.int32, (n_rows, LANES), 1)
        pick = lane == h_row
        run_rows = jnp.concatenate([run] * n_new, axis=0)
        cum_new = jnp.zeros((1, LANES), F32)
        s_new = []
        for u in range(n_new):
            cum_new = cum_new + lf[u:u + 1]
            cu = jnp.sum(jnp.where(pick, cum_new, 0.0), axis=1, keepdims=True)
            su = jnp.sum(qbd * kn[u:u + 1], axis=1, keepdims=True) - (run_rows + cu)
            s_new.append(jnp.where(t_row >= u, su, NEG_INF))
        alpha, p = _online_merge(jnp.concatenate(s_new, axis=1), m_ref, l_ref)
        acc = alpha * acc_ref[...]
        for u in range(n_new):
            acc += p[:, u:u + 1] * vn[u:u + 1]
        o = acc / l_ref[...]
        col = lax.broadcasted_iota(jnp.int32, o.shape, 1)
        o = _slot_sum(jnp.where(col // HEAD_DIM == h_row, o, 0.0), n_new)
        y = o * _silu(gate_ref[...][:n_new])
        o_ref[...] = jnp.concatenate([y, jnp.zeros((ROWS_S - n_new, W_C), F32)], axis=0)


def _dec_c(proj3, cache_kt, cache_vt, cache_f, layer, pt_flat, bf_pad, n_new):
    b = proj3.shape[0]
    page = cache_kt.shape[3]
    n_pages = pt_flat.shape[0] // b
    n_steps = n_pages // PAGES_PER_STEP

    def new_blk(col, w):
        return pl.BlockSpec((None, ROWS_S, w), lambda bb, s, pt: (bb, 0, col // w))

    def page_idx(bb, s, pt, n):
        return pt[bb * n_pages + s * PAGES_PER_STEP + n]

    def kv_blk(n):
        return pl.BlockSpec((None, None, W_C, page),
                            lambda bb, s, pt: (layer, page_idx(bb, s, pt, n), 0, 0))

    def f_blk(n):
        return pl.BlockSpec((None, SUBLANES, SUBLANES, page),
                            lambda bb, s, pt: (layer, 0, page_idx(bb, s, pt, n) // SUBLANES, 0))

    n_rows = n_new * SUBLANES
    return pl.pallas_call(
        functools.partial(_dec_c_kernel, n_new=n_new, page=page, n_pages=n_pages),
        out_shape=(jax.ShapeDtypeStruct((b, ROWS_S, W_C), F32),
                   jax.ShapeDtypeStruct((b, ROWS_S, LANES), F32)),
        grid_spec=pltpu.PrefetchScalarGridSpec(
            num_scalar_prefetch=1,
            grid=(b, n_steps),
            in_specs=[pl.BlockSpec((1, LANES), lambda bb, s, pt: (0, 0)),
                      new_blk(COL['c_q'], W_C), new_blk(COL['c_k'], W_C), new_blk(COL['c_v'], W_C),
                      new_blk(COL['c_f'], LANES), new_blk(COL['c_g'], W_C)]
                     + [kv_blk(n) for n in range(PAGES_PER_STEP)] * 2
                     + [f_blk(n) for n in range(PAGES_PER_STEP)],
            out_specs=(pl.BlockSpec((None, ROWS_S, W_C), lambda bb, s, pt: (bb, 0, 0)),
                       pl.BlockSpec((None, ROWS_S, LANES), lambda bb, s, pt: (bb, 0, 0))),
            scratch_shapes=[pltpu.VMEM((n_rows, W_C), F32),
                            pltpu.VMEM((n_rows, 1), F32),
                            pltpu.VMEM((n_rows, 1), F32),
                            pltpu.VMEM((n_rows, W_C), F32),
                            pltpu.VMEM((SUBLANES, 1), F32)]),
        compiler_params=_cparams(("parallel", "arbitrary")),
        name="dec_c",
    )(pt_flat, bf_pad, proj3, proj3, proj3, proj3, proj3,
      *([cache_kt] * PAGES_PER_STEP), *([cache_vt] * PAGES_PER_STEP), *([cache_f] * PAGES_PER_STEP))


def _pick_tile(n, prefs):
    for t in prefs:
        if n % t == 0:
            return t
    return n


def kernel(x_prompt, x_sample, cache_a_k, cache_a_v, cache_c_k, cache_c_v, cache_c_logf, cache_mem_k, cache_mem_v, state_ret, page_table, mem_prompt, w_in, b_f, lam_q1, lam_k1, lam_q2, lam_k2, a_subln_g, w_out, w_mem_kv, ln_g, ln_b):
    depth, d_model, _ = w_in.shape
    bp, seq, _ = x_prompt.shape
    bs, n_new, _ = x_sample.shape
    n_phys, page = cache_a_k.shape[1], cache_a_k.shape[2]
    n_mem = mem_prompt.shape[1]
    alpha = (2 * depth) ** 0.25
    assert n_new <= ROWS_S and page_table.shape[1] % PAGES_PER_STEP == 0
    assert page % LANES == 0 and n_phys % SUBLANES == 0

    w_perm = _permute_w_in(w_in).astype(BF16)
    w_out_b = w_out.astype(BF16)
    w_mem_b = w_mem_kv.astype(BF16)
    bf_pad = jnp.pad(b_f, ((0, 0), (0, LANES - H_C)))[:, None, :]
    lamv = jnp.stack([lam_q1, lam_k1, lam_q2, lam_k2], axis=1)
    g_a = a_subln_g[:, None, :]
    ln_g3, ln_b3 = ln_g[:, None, :], ln_b[:, None, :]
    slopes = 2.0 ** (-8.0 * jnp.arange(1, H_A + 1, dtype=F32) / H_A)
    ck_a = cache_a_k.reshape(depth, n_phys, page * H_A, 2 * HEAD_DIM)
    cv_a = cache_a_v.reshape(depth, n_phys, page * H_A, 2 * HEAD_DIM)
    ck_c = jnp.transpose(cache_c_k, (0, 1, 3, 4, 2)).reshape(depth, n_phys, W_C, page)
    cv_c = jnp.transpose(cache_c_v, (0, 1, 3, 4, 2)).reshape(depth, n_phys, W_C, page)
    cf_c = jnp.pad(jnp.transpose(cache_c_logf, (0, 3, 1, 2)),
                   ((0, 0), (0, SUBLANES - H_C), (0, 0), (0, 0)))
    pt_flat = page_table.reshape(-1)
    mem2d = mem_prompt.reshape(bp * n_mem, d_model)
    smk = jnp.transpose(cache_mem_k, (0, 1, 3, 4, 2)).reshape(depth * bs, W_M, n_mem)
    smv = jnp.transpose(cache_mem_v, (0, 1, 3, 4, 2)).reshape(depth * bs, W_M, n_mem)

    chunk_p = RET_CHUNK if seq % RET_CHUNK == 0 else seq
    tables_p = _ret_tables(chunk_p, chunk_p)
    tables_s = _ret_tables(ROWS_S, n_new)
    s0_p = jnp.zeros((bp, H_B // 2, LANES, LANES), F32)

    tq = _pick_tile(seq, (512, 256, 128))
    tm_p = _pick_tile(bp * seq, (1024, 512, 256, 128))
    tn = _pick_tile(PW, (1152, 640, 128))
    ret_rows = _pick_tile(seq, (1024, 512, 256, 128))
    tq_m = _pick_tile(seq, (1024, 512, 256, 128))

    xp = x_prompt.reshape(bp * seq, d_model)
    xs = jnp.pad(x_sample, ((0, 0), (0, ROWS_S - n_new), (0, 0))).reshape(bs * ROWS_S, d_model)

    outs_p = [[] for _ in range(8)]
    outs_s = [[] for _ in range(6)]
    for l in range(depth):
        lam_init = 0.8 - 0.6 * math.exp(-0.3 * l)

        pj = _proj(xp, w_perm, l, tm_p, tn).reshape(bp, seq, PW)
        logf_t, cum_t = _logf_cum(pj, bf_pad[l])
        o_a = _attn_a(pj, slopes, lamv[l], g_a[l], lam_init, tq)
        o_c = _attn_c(pj, cum_t, tq)
        o_b, s_new = _retention(pj, s0_p, tables_p, chunk_p, ret_rows)
        mkv = _proj(mem2d, w_mem_b, l, _pick_tile(bp * n_mem, (512, 256, 128)), W_M).reshape(bp, n_mem, 2 * W_M)
        o_m = _mem_attn(pj, mkv, mkv, 0, 1, tq_m, False)
        xp = _out_ln(o_a.reshape(bp * seq, W_A), o_b.reshape(bp * seq, W_B), o_c.reshape(bp * seq, W_C),
                     o_m.reshape(bp * seq, W_M), w_out_b, l, xp, ln_g3, ln_b3, alpha,
                     _pick_tile(bp * seq, (512, 256, 128)))
        outs_p[0].append(pj[:, :, COL['a_k']:COL['a_k'] + W_A].reshape(bp, seq, H_A, 2 * HEAD_DIM))
        outs_p[1].append(pj[:, :, COL['a_v']:COL['a_v'] + W_A].reshape(bp, seq, H_A, 2 * HEAD_DIM))
        outs_p[2].append(pj[:, :, COL['c_k']:COL['c_k'] + W_C].reshape(bp, seq, H_C, HEAD_DIM))
        outs_p[3].append(pj[:, :, COL['c_v']:COL['c_v'] + W_C].reshape(bp, seq, H_C, HEAD_DIM))
        outs_p[4].append(jnp.transpose(logf_t[:, :H_C, :], (0, 2, 1)))
        outs_p[5].append(_state_from_blockdiag(s_new))
        outs_p[6].append(mkv[:, :, :W_M].reshape(bp, n_mem, H_M, HEAD_DIM))
        outs_p[7].append(mkv[:, :, W_M:].reshape(bp, n_mem, H_M, HEAD_DIM))

        pjs = _proj(xs, w_perm, l, _pick_tile(bs * ROWS_S, (256, 128, 8)), tn).reshape(bs, ROWS_S, PW)
        o_a = _dec_a(pjs, ck_a, cv_a, l, pt_flat, slopes, lamv[l], g_a[l], lam_init, n_new)
        o_c, logf_s = _dec_c(pjs, ck_c, cv_c, cf_c, l, pt_flat, bf_pad[l], n_new)
        o_b, s_new = _retention(pjs, _state_to_blockdiag(state_ret[l]), tables_s, ROWS_S, ROWS_S)
        o_m = _mem_attn(pjs, smk, smv, 0, 0, ROWS_S, True, mem_off=l * bs)
        xs = _out_ln(o_a.reshape(bs * ROWS_S, W_A), o_b.reshape(bs * ROWS_S, W_B),
                     o_c.reshape(bs * ROWS_S, W_C), o_m.reshape(bs * ROWS_S, W_M), w_out_b, l, xs,
                     ln_g3, ln_b3, alpha, _pick_tile(bs * ROWS_S, (256, 128, 8)))
        outs_s[0].append(pjs[:, :n_new, COL['a_k']:COL['a_k'] + W_A].reshape(bs, n_new, H_A, 2 * HEAD_DIM))
        outs_s[1].append(pjs[:, :n_new, COL['a_v']:COL['a_v'] + W_A].reshape(bs, n_new, H_A, 2 * HEAD_DIM))
        outs_s[2].append(pjs[:, :n_new, COL['c_k']:COL['c_k'] + W_C].reshape(bs, n_new, H_C, HEAD_DIM))
        outs_s[3].append(pjs[:, :n_new, COL['c_v']:COL['c_v'] + W_C].reshape(bs, n_new, H_C, HEAD_DIM))
        outs_s[4].append(logf_s[:, :n_new, :H_C])
        outs_s[5].append(_state_from_blockdiag(s_new))

    y_p = xp.reshape(bp, seq, d_model)
    y_s = xs.reshape(bs, ROWS_S, d_model)[:, :n_new]
    return (y_p, y_s, *(jnp.stack(o) for o in outs_p), *(jnp.stack(o) for o in outs_s))
```

```python
import functools
import math

import jax
import jax.numpy as jnp
import numpy as np
from jax import lax
from jax.experimental import pallas as pl
from jax.experimental.pallas import tpu as pltpu

F32 = jnp.float32
BF16 = jnp.bfloat16

HEAD_DIM = 64
H_A, H_B, H_C, H_M = 4, 6, 6, 4
W_A, W_B, W_C, W_M = H_A * 2 * HEAD_DIM, H_B * HEAD_DIM, H_C * HEAD_DIM, H_M * HEAD_DIM
MIX = W_A + W_B + W_C + W_M
LANES = 128
SUBLANES = 8
RET_CHUNK = 128
LN_EPS = 1e-5
NEG_INF = -1e30
QK_SCALE = HEAD_DIM ** -0.5
VMEM_LIMIT = 48 * 1024 * 1024

_REF_GROUPS = (('a_q', W_A), ('a_k', W_A), ('a_v', W_A), ('a_g', W_A),
               ('b_q', W_B), ('b_k', W_B), ('b_v', W_B), ('b_g', W_B),
               ('c_q', W_C), ('c_k', W_C), ('c_v', W_C), ('c_g', W_C), ('c_f', H_C),
               ('m_q', W_M), ('m_g', W_M))
_GROUPS = (('a_q', W_A), ('a_k', W_A), ('a_v', W_A), ('a_g', W_A), ('m_q', W_M),
           ('c_q', W_C), ('c_g', W_C), ('b_q', W_B), ('b_k', W_B), ('b_v', W_B), ('b_g', W_B),
           ('m_g', W_M), ('c_f', LANES), ('c_k', W_C), ('c_v', W_C))
REF_COL, COL = {}, {}
_o = 0
for _n, _w in _REF_GROUPS:
    REF_COL[_n] = (_o, _w)
    _o += _w
_o = 0
for _n, _w in _GROUPS:
    assert _o % _w == 0 and _o % LANES == 0
    COL[_n] = _o
    _o += _w
PW = _o
PW_MAIN = COL['c_k']
BLK = {n: o // LANES for n, o in COL.items()}
LOG2E = math.log2(math.e)


def _permute_w_in(w_in):
    parts = []
    for n, w in _GROUPS:
        o, wr = REF_COL[n]
        parts.append(w_in[:, :, o:o + wr])
        if w > wr:
            parts.append(jnp.zeros(w_in.shape[:2] + (w - wr,), w_in.dtype))
    return jnp.concatenate(parts, axis=-1)


def _cparams(sem):
    return pltpu.CompilerParams(dimension_semantics=sem, vmem_limit_bytes=VMEM_LIMIT)


def _silu(x):
    return x / (1.0 + jnp.exp(-x))


def _dot(a, b):
    return jnp.dot(a.astype(BF16), b.astype(BF16), preferred_element_type=F32)


def _dot_nt(a, b):
    return lax.dot_general(a.astype(BF16), b.astype(BF16), (((1,), (1,)), ((), ())),
                           preferred_element_type=F32)


def _lane_half_masks(shape):
    lane = lax.broadcasted_iota(jnp.int32, shape, len(shape) - 1)
    lo = (lane % LANES) < HEAD_DIM
    return lo, jnp.logical_not(lo)


def _proj_kernel(x_ref, w_ref, o_ref, xb_ref):
    @pl.when(pl.program_id(1) == 0)
    def _():
        xb_ref[...] = x_ref[...].astype(BF16)

    o_ref[...] = jnp.dot(xb_ref[...], w_ref[...], preferred_element_type=F32)


def _proj(x2d, w_all, layer, tm, tn, n_cols=None):
    m, k = x2d.shape
    n = w_all.shape[2] if n_cols is None else n_cols
    return pl.pallas_call(
        _proj_kernel,
        out_shape=jax.ShapeDtypeStruct((m, n), F32),
        grid=(m // tm, n // tn),
        in_specs=[pl.BlockSpec((tm, k), lambda i, j: (i, 0)),
                  pl.BlockSpec((None, k, tn), lambda i, j: (layer, 0, j))],
        out_specs=pl.BlockSpec((tm, tn), lambda i, j: (i, j)),
        scratch_shapes=[pltpu.VMEM((tm, k), BF16)],
        compiler_params=_cparams(("parallel", "arbitrary")),
        name="proj",
    )(x2d, w_all)


def _proj_t_kernel(wt_ref, x_ref, o_ref):
    o_ref[...] = _dot_nt(wt_ref[...], x_ref[...])


def _proj_t(x3d, wt_all, layer, tm):
    b, seq, k = x3d.shape
    n = wt_all.shape[1]
    return pl.pallas_call(
        _proj_t_kernel,
        out_shape=jax.ShapeDtypeStruct((b, n, seq), F32),
        grid=(b, seq // tm),
        in_specs=[pl.BlockSpec((None, n, k), lambda bb, i: (layer, 0, 0)),
                  pl.BlockSpec((None, tm, k), lambda bb, i: (bb, i, 0))],
        out_specs=pl.BlockSpec((None, n, tm), lambda bb, i: (bb, 0, i)),
        compiler_params=_cparams(("parallel", "parallel")),
        name="proj_t",
    )(wt_all, x3d)


def _log_sigmoid(z):
    return jnp.minimum(z, 0.0) - jnp.log1p(jnp.exp(-jnp.abs(z)))


def _lane_cumsum(x):
    n = x.shape[-1]
    lane = lax.broadcasted_iota(jnp.int32, x.shape, x.ndim - 1)
    shift = 1
    while shift < n:
        x = x + jnp.where(lane >= shift, pltpu.roll(x, shift, x.ndim - 1), 0.0)
        shift *= 2
    return x


def _logf_cum_kernel(cf_ref, bf_ref, logf_ref, cum_ref):
    lf = _log_sigmoid(cf_ref[...] + bf_ref[...])
    lf_t = lf.T[:SUBLANES, :]
    logf_ref[...] = lf_t
    cum_ref[...] = _lane_cumsum(lf_t)


def _logf_cum(proj3, bf_pad):
    b, seq, _ = proj3.shape
    out = jax.ShapeDtypeStruct((b, SUBLANES, seq), F32)
    return pl.pallas_call(
        _logf_cum_kernel,
        out_shape=(out, out),
        grid=(b,),
        in_specs=[pl.BlockSpec((None, seq, LANES), lambda i: (i, 0, BLK['c_f'])),
                  pl.BlockSpec((1, LANES), lambda i: (0, 0))],
        out_specs=(pl.BlockSpec((None, SUBLANES, seq), lambda i: (i, 0, 0)),
                   pl.BlockSpec((None, SUBLANES, seq), lambda i: (i, 0, 0))),
        compiler_params=_cparams(("parallel",)),
        name="logf_cum",
    )(proj3, bf_pad)


def _tri_pairs(n):
    qi = np.array([i for i in range(n) for _ in range(i + 1)], np.int32)
    kj = np.array([j for i in range(n) for j in range(i + 1)], np.int32)
    return jnp.asarray(qi), jnp.asarray(kj)


SOFTMAX_ROWS = 32


def _stack_queries(q_ref, q2_ref, tq):
    q = q_ref[...] * (QK_SCALE * LOG2E)
    lo, hi = _lane_half_masks(q.shape)
    q2_ref[0:tq, :] = jnp.where(lo, q, 0.0).astype(BF16)
    q2_ref[tq:, :] = jnp.where(hi, q, 0.0).astype(BF16)


def _flash_init(m_ref, l_ref, acc_ref):
    m_ref[...] = jnp.full(m_ref.shape, NEG_INF, F32)
    l_ref[...] = jnp.zeros(l_ref.shape, F32)
    acc_ref[...] = jnp.zeros(acc_ref.shape, F32)


def _softmax_tile(s_ref, p_ref, m_ref, l_ref, alpha_ref, col_bias, tq, causal):
    tk = s_ref.shape[1]
    rc = SOFTMAX_ROWS
    reps = tk // LANES
    col = lax.broadcasted_iota(jnp.int32, (rc, tk), 1) if causal else None
    row0 = lax.broadcasted_iota(jnp.int32, (rc, tk), 0) if causal else None
    for c in range(2 * tq // rc):
        rows = slice(c * rc, (c + 1) * rc)
        s = s_ref[rows, :] + col_bias(c * rc // tq)
        if causal:
            s = jnp.where(col <= row0 + (c * rc) % tq, s, NEG_INF)
        m_old = m_ref[rows, :]
        m_new = jnp.maximum(m_old, jnp.max(s, axis=1, keepdims=True))
        alpha = jnp.exp2(m_old - m_new)
        p = jnp.exp2(s - jnp.concatenate([m_new] * reps, axis=1))
        l_ref[rows, :] = alpha * l_ref[rows, :] + jnp.sum(p, axis=1, keepdims=True)
        m_ref[rows, :] = m_new
        alpha_ref[rows, :] = alpha
        p_ref[rows, :] = p.astype(BF16)


def _lambda(lamv_ref, lam_init):
    lv = lamv_ref[...]
    e1 = jnp.exp(jnp.sum(lv[0:1] * lv[1:2], axis=1, keepdims=True))
    e2 = jnp.exp(jnp.sum(lv[2:3] * lv[3:4], axis=1, keepdims=True))
    return e1 - e2 + lam_init


def _attn_a_kernel(qi_ref, kj_ref, slopes_ref, lamv_ref, g_ref, q_ref, k_ref, v_ref, gate_ref,
                   o_ref, q2_ref, s_ref, p_ref, m_ref, l_ref, alpha_ref, acc_ref, *, tq, lam_init):
    h = pl.program_id(1)
    p = pl.program_id(2)
    i = qi_ref[p]
    j = kj_ref[p]

    @pl.when(j == 0)
    def _():
        _stack_queries(q_ref, q2_ref, tq)
        _flash_init(m_ref, l_ref, acc_ref)

    def step(causal):
        s_ref[...] = _dot_nt(q2_ref[...], k_ref[...])
        kpos = lax.broadcasted_iota(jnp.int32, (1, s_ref.shape[1]), 1) + j * tq
        bias = (slopes_ref[h] * LOG2E) * kpos.astype(F32)
        _softmax_tile(s_ref, p_ref, m_ref, l_ref, alpha_ref, lambda half: bias, tq, causal)
        acc_ref[...] = alpha_ref[...] * acc_ref[...] + _dot(p_ref[...], v_ref[...])

    @pl.when(j < i)
    def _():
        step(False)

    @pl.when(j == i)
    def _():
        step(True)
        o = acc_ref[...] / l_ref[...]
        o = o[:tq] - _lambda(lamv_ref, lam_init) * o[tq:]
        y = o * lax.rsqrt(jnp.mean(o * o, axis=1, keepdims=True) + LN_EPS) * g_ref[...]
        o_ref[...] = y * (1.0 - lam_init) * _silu(gate_ref[...])


def _attn_a(proj3, slopes, lamv, g, lam_init, tq):
    b, seq, _ = proj3.shape
    qi, kj = _tri_pairs(seq // tq)

    def blk(col, tile):
        return pl.BlockSpec((None, tq, LANES),
                            lambda bb, h, p, qi, kj: (bb, (qi if tile == 'q' else kj)[p], col + h))

    kern = functools.partial(_attn_a_kernel, tq=tq, lam_init=lam_init)
    return pl.pallas_call(
        kern,
        out_shape=jax.ShapeDtypeStruct((b, seq, W_A), F32),
        grid_spec=pltpu.PrefetchScalarGridSpec(
            num_scalar_prefetch=2,
            grid=(b, H_A, qi.shape[0]),
            in_specs=[pl.BlockSpec(memory_space=pltpu.SMEM),
                      pl.BlockSpec((4, HEAD_DIM), lambda bb, h, p, qi, kj: (0, 0)),
                      pl.BlockSpec((1, LANES), lambda bb, h, p, qi, kj: (0, 0)),
                      blk(BLK['a_q'], 'q'), blk(BLK['a_k'], 'k'), blk(BLK['a_v'], 'k'),
                      blk(BLK['a_g'], 'q')],
            out_specs=pl.BlockSpec((None, tq, LANES), lambda bb, h, p, qi, kj: (bb, qi[p], h)),
            scratch_shapes=_flash_scratch(tq)),
        compiler_params=_cparams(("parallel", "parallel", "arbitrary")),
        name="attn_a",
    )(qi, kj, slopes, lamv, g, proj3, proj3, proj3, proj3)


def _flash_scratch(tq):
    rep = pltpu.VMEM((2 * tq, LANES), F32)
    return [pltpu.VMEM((2 * tq, LANES), BF16),
            pltpu.VMEM((2 * tq, tq), F32),
            pltpu.VMEM((2 * tq, tq), BF16),
            rep, rep, rep, rep]


def _attn_c_kernel(qi_ref, kj_ref, q_ref, kt_ref, vt_ref, cum_ref, gate_ref,
                   o_ref, q2_ref, s_ref, p_ref, m_ref, l_ref, alpha_ref, acc_ref, *, tq):
    hb = pl.program_id(1)
    p = pl.program_id(2)
    i = qi_ref[p]
    j = kj_ref[p]

    @pl.when(j == 0)
    def _():
        _stack_queries(q_ref, q2_ref, tq)
        _flash_init(m_ref, l_ref, acc_ref)

    def step(causal):
        s_ref[...] = _dot(q2_ref[...], kt_ref[...])
        bias = [cum_ref[pl.ds(2 * hb + half, 1), :] * (-LOG2E) for half in range(2)]
        _softmax_tile(s_ref, p_ref, m_ref, l_ref, alpha_ref, lambda half: bias[half], tq, causal)
        acc_ref[...] = alpha_ref[...] * acc_ref[...] + _dot_nt(p_ref[...], vt_ref[...])

    @pl.when(j < i)
    def _():
        step(False)

    @pl.when(j == i)
    def _():
        step(True)
        o = acc_ref[...] / l_ref[...]
        lo, _ = _lane_half_masks((tq, LANES))
        o_ref[...] = jnp.where(lo, o[:tq], o[tq:]) * _silu(gate_ref[...])


def _attn_c(proj3, kvt, cum_t, tq):
    b, seq, _ = proj3.shape
    qi, kj = _tri_pairs(seq // tq)
    n_pairs = H_C // 2

    def blk(col):
        return pl.BlockSpec((None, tq, LANES), lambda bb, h, p, qi, kj: (bb, qi[p], col + h))

    def blk_t(row):
        return pl.BlockSpec((None, LANES, tq), lambda bb, h, p, qi, kj: (bb, row + h, kj[p]))

    return pl.pallas_call(
        functools.partial(_attn_c_kernel, tq=tq),
        out_shape=jax.ShapeDtypeStruct((b, seq, W_C), F32),
        grid_spec=pltpu.PrefetchScalarGridSpec(
            num_scalar_prefetch=2,
            grid=(b, n_pairs, qi.shape[0]),
            in_specs=[blk(BLK['c_q']), blk_t(0), blk_t(n_pairs),
                      pl.BlockSpec((None, SUBLANES, tq), lambda bb, h, p, qi, kj: (bb, 0, kj[p])),
                      blk(BLK['c_g'])],
            out_specs=pl.BlockSpec((None, tq, LANES), lambda bb, h, p, qi, kj: (bb, qi[p], h)),
            scratch_shapes=_flash_scratch(tq)),
        compiler_params=_cparams(("parallel", "parallel", "arbitrary")),
        name="attn_c",
    )(qi, kj, proj3, kvt, kvt, cum_t, proj3)


def _ret_tables(chunk, n_valid):
    log_gamma = jnp.log(1.0 - 2.0 ** (-5.0 - jnp.arange(H_B, dtype=F32)))
    idx = jnp.arange(chunk, dtype=F32)
    valid = idx < n_valid
    rel = idx[:, None] - idx[None, :]
    decay = jnp.where(rel >= 0, jnp.exp(log_gamma[:, None, None] * jnp.maximum(rel, 0.0)), 0.0)
    key_rows = max(chunk, LANES)
    decay = jnp.pad(decay, ((0, 0), (0, 0), (0, key_rows - chunk)))
    k_w = jnp.where(valid, jnp.exp(log_gamma[:, None] * (n_valid - 1 - idx)[None, :]), 0.0)
    q_w = jnp.exp(log_gamma[:, None] * (idx + 1)[None, :])
    g = jnp.exp(log_gamma * n_valid)

    def lanes(t):
        t = t.reshape(H_B // 2, 2, chunk)
        return jnp.repeat(jnp.transpose(t, (0, 2, 1)), HEAD_DIM, axis=2)

    g_rows = jnp.repeat(g.reshape(H_B // 2, 2), HEAD_DIM, axis=1)
    g_tab = jnp.broadcast_to(g_rows[:, :, None], (H_B // 2, LANES, LANES))
    return decay.reshape(H_B // 2, 2, chunk, key_rows), lanes(k_w) * QK_SCALE, lanes(q_w), g_tab


def _retention_kernel(q_ref, k_ref, v_ref, gate_ref, s0_ref, decay_ref, kw_ref, qw_ref, g_ref,
                      o_ref, s_out_ref, s_ref, *, chunk, n_chunks):
    t = pl.program_id(2)

    @pl.when(t == 0)
    def _():
        s_ref[...] = s0_ref[...]

    shape = (chunk, LANES)
    lo, hi = _lane_half_masks(shape)
    r = lax.broadcasted_iota(jnp.int32, (LANES, LANES), 0)
    c = lax.broadcasted_iota(jnp.int32, (LANES, LANES), 1)
    same_head = (r < HEAD_DIM) == (c < HEAD_DIM)

    def body(n, carry):
        rows = pl.ds(pl.multiple_of(n * chunk, chunk), chunk)
        q = q_ref[rows, :]
        k = k_ref[rows, :]
        v = v_ref[rows, :]
        kw = k * kw_ref[...]
        if chunk < LANES:
            pad = jnp.zeros((LANES - chunk, LANES), F32)
            k, v, kw = (jnp.concatenate([a, pad], axis=0) for a in (k, v, kw))
        lo_k, hi_k = _lane_half_masks(v.shape)
        state = s_ref[...]
        o_cross = _dot(q * qw_ref[...], state)
        a_lo = _dot_nt(jnp.where(lo, q, 0.0), k) * (QK_SCALE * decay_ref[0])
        a_hi = _dot_nt(jnp.where(hi, q, 0.0), k) * (QK_SCALE * decay_ref[1])
        o = o_cross + _dot(a_lo, jnp.where(lo_k, v, 0.0)) + _dot(a_hi, jnp.where(hi_k, v, 0.0))
        kv = _dot(kw.T, v)
        s_ref[...] = g_ref[...] * state + jnp.where(same_head, kv, 0.0)
        sq = o * o
        ms_lo = jnp.sum(jnp.where(lo, sq, 0.0), axis=1, keepdims=True) * (1.0 / HEAD_DIM)
        ms_hi = jnp.sum(jnp.where(hi, sq, 0.0), axis=1, keepdims=True) * (1.0 / HEAD_DIM)
        y = o * lax.rsqrt(jnp.where(lo, ms_lo, ms_hi) + LN_EPS)
        o_ref[rows, :] = y * _silu(gate_ref[rows, :])
        return carry

    lax.fori_loop(0, n_chunks, body, 0)

    @pl.when(t == pl.num_programs(2) - 1)
    def _():
        s_out_ref[...] = s_ref[...]


def _retention(proj3, s0_bd, tables, chunk, rows_per_step):
    b, seq, _ = proj3.shape
    decay, kw, qw, g_tab = tables
    n_pairs = H_B // 2
    n_chunks = rows_per_step // chunk

    def blk(col):
        return pl.BlockSpec((None, rows_per_step, LANES), lambda bb, h, t: (bb, t, col + h))

    def tab(shape):
        return pl.BlockSpec((None,) + shape, lambda bb, h, t: (h,) + (0,) * len(shape))

    return pl.pallas_call(
        functools.partial(_retention_kernel, chunk=chunk, n_chunks=n_chunks),
        out_shape=(jax.ShapeDtypeStruct((b, seq, W_B), F32),
                   jax.ShapeDtypeStruct((b, n_pairs, LANES, LANES), F32)),
        grid=(b, n_pairs, seq // rows_per_step),
        in_specs=[blk(BLK['b_q']), blk(BLK['b_k']), blk(BLK['b_v']), blk(BLK['b_g']),
                  pl.BlockSpec((None, None, LANES, LANES), lambda bb, h, t: (bb, h, 0, 0)),
                  tab((2, chunk, max(chunk, LANES))), tab((chunk, LANES)), tab((chunk, LANES)),
                  tab((LANES, LANES))],
        out_specs=(pl.BlockSpec((None, rows_per_step, LANES), lambda bb, h, t: (bb, t, h)),
                   pl.BlockSpec((None, None, LANES, LANES), lambda bb, h, t: (bb, h, 0, 0))),
        scratch_shapes=[pltpu.VMEM((LANES, LANES), F32)],
        compiler_params=_cparams(("parallel", "parallel", "arbitrary")),
        name="retention",
    )(proj3, proj3, proj3, proj3, s0_bd, decay, kw, qw, g_tab)


def _state_to_blockdiag(s):
    b = s.shape[0]
    s = s.reshape(b, H_B // 2, 2, HEAD_DIM, HEAD_DIM)
    z = jnp.zeros_like(s[:, :, 0])
    top = jnp.concatenate([s[:, :, 0], z], axis=-1)
    bot = jnp.concatenate([z, s[:, :, 1]], axis=-1)
    return jnp.concatenate([top, bot], axis=-2)


def _state_from_blockdiag(s):
    b = s.shape[0]
    even = s[:, :, :HEAD_DIM, :HEAD_DIM]
    odd = s[:, :, HEAD_DIM:, HEAD_DIM:]
    return jnp.stack([even, odd], axis=2).reshape(b, H_B, HEAD_DIM, HEAD_DIM)


def _mem_attn_kernel(q_ref, gate_ref, mk_ref, mv_ref, o_ref):
    q = q_ref[...] * QK_SCALE
    lo, hi = _lane_half_masks((q.shape[0], LANES))
    outs = []
    for pair in range(H_M // 2):
        cols = slice(pair * LANES, (pair + 1) * LANES)
        qp = q[:, cols]
        mk = mk_ref[:, cols]
        mv = mv_ref[:, cols]
        o_pair = None
        for mask in (lo, hi):
            s = _dot_nt(jnp.where(mask, qp, 0.0), mk)
            p = jnp.exp(s - jnp.max(s, axis=1, keepdims=True))
            o = _dot(p, mv) / jnp.sum(p, axis=1, keepdims=True)
            o_pair = jnp.where(mask, o, 0.0) if o_pair is None else jnp.where(mask, o, o_pair)
        outs.append(o_pair)
    o_ref[...] = jnp.concatenate(outs, axis=1) * _silu(gate_ref[...])


def _mem_attn_t_kernel(q_ref, gate_ref, mkt_ref, mvt_ref, o_ref):
    q = q_ref[...] * QK_SCALE
    col = lax.broadcasted_iota(jnp.int32, q.shape, 1)
    o_all = None
    for h in range(H_M):
        head = col // HEAD_DIM == h
        s = _dot(jnp.where(head, q, 0.0), mkt_ref[...])
        p = jnp.exp(s - jnp.max(s, axis=1, keepdims=True))
        o = _dot_nt(p, mvt_ref[...]) / jnp.sum(p, axis=1, keepdims=True)
        o_all = jnp.where(head, o, 0.0) if o_all is None else jnp.where(head, o, o_all)
    o_ref[...] = o_all * _silu(gate_ref[...])


def _mem_attn(proj3, mk_arr, mv_arr, mk_blk, mv_blk, tq, transposed, mem_off=0):
    b, seq, _ = proj3.shape
    mem_block = (None, W_M, mk_arr.shape[2]) if transposed else (None, mk_arr.shape[1], W_M)
    return pl.pallas_call(
        _mem_attn_t_kernel if transposed else _mem_attn_kernel,
        out_shape=jax.ShapeDtypeStruct((b, seq, W_M), F32),
        grid=(b, seq // tq),
        in_specs=[pl.BlockSpec((None, tq, W_M), lambda bb, i: (bb, i, COL['m_q'] // W_M)),
                  pl.BlockSpec((None, tq, W_M), lambda bb, i: (bb, i, COL['m_g'] // W_M)),
                  pl.BlockSpec(mem_block, lambda bb, i: (bb + mem_off, 0, mk_blk)),
                  pl.BlockSpec(mem_block, lambda bb, i: (bb + mem_off, 0, mv_blk))],
        out_specs=pl.BlockSpec((None, tq, W_M), lambda bb, i: (bb, i, 0)),
        compiler_params=_cparams(("parallel", "parallel")),
        name="mem_attn",
    )(proj3, proj3, mk_arr, mv_arr)


def _out_ln_kernel(oa_ref, ob_ref, oc_ref, om_ref, w_ref, x_ref, g_ref, b_ref, y_ref, *, alpha):
    y = _dot(oa_ref[...], w_ref[0:W_A, :])
    y += _dot(ob_ref[...], w_ref[W_A:W_A + W_B, :])
    y += _dot(oc_ref[...], w_ref[W_A + W_B:W_A + W_B + W_C, :])
    y += _dot(om_ref[...], w_ref[W_A + W_B + W_C:, :])
    z = alpha * x_ref[...] + y
    mu = jnp.mean(z, axis=1, keepdims=True)
    zc = z - mu
    var = jnp.mean(zc * zc, axis=1, keepdims=True)
    y_ref[...] = zc * lax.rsqrt(var + LN_EPS) * g_ref[...] + b_ref[...]


def _out_ln(o_a, o_b, o_c, o_m, w_out_all, layer, x2d, ln_g, ln_b, alpha, tm):
    m, d = x2d.shape

    def rows(w):
        return pl.BlockSpec((tm, w), lambda i: (i, 0))

    return pl.pallas_call(
        functools.partial(_out_ln_kernel, alpha=alpha),
        out_shape=jax.ShapeDtypeStruct((m, d), F32),
        grid=(m // tm,),
        in_specs=[rows(W_A), rows(W_B), rows(W_C), rows(W_M),
                  pl.BlockSpec((None, MIX, d), lambda i: (layer, 0, 0)),
                  rows(d),
                  pl.BlockSpec((None, 1, d), lambda i: (layer, 0, 0)),
                  pl.BlockSpec((None, 1, d), lambda i: (layer, 0, 0))],
        out_specs=rows(d),
        compiler_params=_cparams(("parallel",)),
        name="out_ln",
    )(o_a, o_b, o_c, o_m, w_out_all, x2d, ln_g, ln_b)


PAGES_PER_STEP = 16
ROWS_S = 8


def _split3(x):
    hi = x.astype(BF16).astype(F32)
    mid = (x - hi).astype(BF16).astype(F32)
    lo = x - hi - mid
    return hi, mid, lo


def _online_merge(s, m_ref, l_ref):
    m_old = m_ref[...]
    m_new = jnp.maximum(m_old, jnp.max(s, axis=1, keepdims=True))
    alpha = jnp.exp(m_old - m_new)
    p = jnp.exp(s - m_new)
    l_ref[...] = alpha * l_ref[...] + jnp.sum(p, axis=1, keepdims=True)
    m_ref[...] = m_new
    return alpha, p


def _rows_from_tokens(x, n_new):
    return jnp.concatenate([jnp.broadcast_to(x[t:t + 1], (SUBLANES, x.shape[1])) for t in range(n_new)],
                           axis=0)


def _slot_sum(x, n_new):
    return jnp.sum(x.reshape(n_new, SUBLANES, x.shape[1]), axis=1)


def _dec_a_kernel(pt_ref, slopes_ref, lamv_ref, g_ref, q_ref, kn_ref, vn_ref, gate_ref, *rest,
                  n_new, n_past, page, lam_init):
    k_pages = rest[:PAGES_PER_STEP]
    v_pages = rest[PAGES_PER_STEP:2 * PAGES_PER_STEP]
    o_ref, q_rows_ref, m_ref, l_ref, acc_ref = rest[2 * PAGES_PER_STEP:]
    g = pl.program_id(1)
    n_rows = n_new * SUBLANES
    row = lax.broadcasted_iota(jnp.int32, (n_rows, 1), 0)
    t_row = row // SUBLANES
    h_row = (row % SUBLANES) // 2
    slope_row = jnp.zeros((n_rows, 1), F32)
    for h in range(H_A):
        slope_row = jnp.where(h_row == h, slopes_ref[h], slope_row)

    def head_rows(x):
        out = jnp.zeros((n_rows, LANES), F32)
        for h in range(H_A):
            out = jnp.where(h_row == h, _rows_from_tokens(x[:, h * LANES:(h + 1) * LANES], n_new), out)
        return out

    @pl.when(g == 0)
    def _():
        q = head_rows(q_ref[...] * QK_SCALE)
        lane = lax.broadcasted_iota(jnp.int32, q.shape, 1)
        q_rows_ref[...] = jnp.where(lane // HEAD_DIM == row % 2, q, 0.0)
        _flash_init(m_ref, l_ref, acc_ref)

    q_rows = q_rows_ref[...]
    col = lax.broadcasted_iota(jnp.int32, (n_rows, page * H_A), 1)
    own = (col % H_A) == h_row
    rel = (col // H_A - (n_past + t_row)).astype(F32)
    bias0 = jnp.where(own, slope_row * rel, NEG_INF)
    s_parts = []
    for n, kp in enumerate(k_pages):
        base = ((g * PAGES_PER_STEP + n) * page).astype(F32)
        s_parts.append(_dot_nt(q_rows, kp[...]) + (bias0 + slope_row * base))
    alpha, p = _online_merge(jnp.concatenate(s_parts, axis=1), m_ref, l_ref)
    acc = alpha * acc_ref[...]
    w = page * H_A
    for n, vp in enumerate(v_pages):
        acc += _dot(p[:, n * w:(n + 1) * w], vp[...])
    acc_ref[...] = acc

    @pl.when(g == pl.num_programs(1) - 1)
    def _():
        kn = kn_ref[...]
        vn = vn_ref[...]
        s_new = []
        for u in range(n_new):
            ku = head_rows(jnp.broadcast_to(kn[u:u + 1], (n_new, W_A)))
            su = jnp.sum(q_rows * ku, axis=1, keepdims=True) + slope_row * (u - t_row).astype(F32)
            s_new.append(jnp.where(t_row >= u, su, NEG_INF))
        alpha, p = _online_merge(jnp.concatenate(s_new, axis=1), m_ref, l_ref)
        acc = alpha * acc_ref[...]
        for u in range(n_new):
            acc += p[:, u:u + 1] * head_rows(jnp.broadcast_to(vn[u:u + 1], (n_new, W_A)))
        o = acc / l_ref[...]
        o = o * jnp.where(row % 2 == 0, 1.0, -_lambda(lamv_ref, lam_init))
        outs = []
        for h in range(H_A):
            oh = _slot_sum(jnp.where(h_row == h, o, 0.0), n_new)
            yh = oh * lax.rsqrt(jnp.mean(oh * oh, axis=1, keepdims=True) + LN_EPS) * g_ref[...]
            outs.append(yh * (1.0 - lam_init))
        y = jnp.concatenate(outs, axis=1) * _silu(gate_ref[...][:n_new])
        o_ref[...] = jnp.concatenate([y, jnp.zeros((ROWS_S - n_new, W_A), F32)], axis=0)


def _dec_a(proj3, cache_k, cache_v, layer, pt_flat, slopes, lamv, g, lam_init, n_new):
    b = proj3.shape[0]
    page = cache_k.shape[2] // H_A
    n_pages = pt_flat.shape[0] // b
    n_steps = n_pages // PAGES_PER_STEP

    def new_blk(col):
        return pl.BlockSpec((None, ROWS_S, W_A), lambda bb, s, pt: (bb, 0, col // W_A))

    def page_blk(n):
        return pl.BlockSpec((None, None, page * H_A, LANES),
                            lambda bb, s, pt: (layer, pt[bb * n_pages + s * PAGES_PER_STEP + n], 0, 0))

    n_rows = n_new * SUBLANES
    kern = functools.partial(_dec_a_kernel, n_new=n_new, n_past=n_pages * page, page=page,
                             lam_init=lam_init)
    return pl.pallas_call(
        kern,
        out_shape=jax.ShapeDtypeStruct((b, ROWS_S, W_A), F32),
        grid_spec=pltpu.PrefetchScalarGridSpec(
            num_scalar_prefetch=1,
            grid=(b, n_steps),
            in_specs=[pl.BlockSpec(memory_space=pltpu.SMEM),
                      pl.BlockSpec((4, HEAD_DIM), lambda bb, s, pt: (0, 0)),
                      pl.BlockSpec((1, LANES), lambda bb, s, pt: (0, 0)),
                      new_blk(COL['a_q']), new_blk(COL['a_k']), new_blk(COL['a_v']),
                      new_blk(COL['a_g'])]
                     + [page_blk(n) for n in range(PAGES_PER_STEP)] * 2,
            out_specs=pl.BlockSpec((None, ROWS_S, W_A), lambda bb, s, pt: (bb, 0, 0)),
            scratch_shapes=[pltpu.VMEM((n_rows, LANES), F32),
                            pltpu.VMEM((n_rows, 1), F32),
                            pltpu.VMEM((n_rows, 1), F32),
                            pltpu.VMEM((n_rows, LANES), F32)]),
        compiler_params=_cparams(("parallel", "arbitrary")),
        name="dec_a",
    )(pt_flat, slopes, lamv, g, proj3, proj3, proj3, proj3,
      *([cache_k] * PAGES_PER_STEP), *([cache_v] * PAGES_PER_STEP))


def _dec_c_kernel(pt_ref, bf_ref, q_ref, kn_ref, vn_ref, cf_ref, gate_ref, *rest,
                  n_new, page, n_pages):
    k_pages = rest[:PAGES_PER_STEP]
    v_pages = rest[PAGES_PER_STEP:2 * PAGES_PER_STEP]
    f_pages = rest[2 * PAGES_PER_STEP:3 * PAGES_PER_STEP]
    o_ref, logf_ref, qbd_ref, m_ref, l_ref, acc_ref, run_ref = rest[3 * PAGES_PER_STEP:]
    b = pl.program_id(0)
    g = pl.program_id(1)
    n_rows = n_new * SUBLANES
    row = lax.broadcasted_iota(jnp.int32, (n_rows, 1), 0)
    t_row = row // SUBLANES
    h_row = row % SUBLANES

    @pl.when(g == 0)
    def _():
        qr = _rows_from_tokens(q_ref[...] * QK_SCALE, n_new)
        col = lax.broadcasted_iota(jnp.int32, (n_rows, W_C), 1)
        qbd_ref[...] = jnp.where(col // HEAD_DIM == h_row, qr, 0.0)
        _flash_init(m_ref, l_ref, acc_ref)
        run_ref[...] = jnp.zeros(run_ref.shape, F32)

    qbd = qbd_ref[...]
    n_r = PAGES_PER_STEP * SUBLANES
    lf = jnp.concatenate([fp[...] for fp in f_pages], axis=0)
    tok_r = lax.broadcasted_iota(jnp.int32, (page, page), 0)
    tok_c = lax.broadcasted_iota(jnp.int32, (page, page), 1)
    y = _dot(jnp.concatenate(_split3(lf), axis=0), jnp.where(tok_r <= tok_c, 1.0, 0.0))
    cum = y[:n_r] + y[n_r:2 * n_r] + y[2 * n_r:]
    r = lax.broadcasted_iota(jnp.int32, (n_r, n_r), 0)
    c = lax.broadcasted_iota(jnp.int32, (n_r, n_r), 1)
    earlier = jnp.where((r % SUBLANES == c % SUBLANES) & (c < r), 1.0, 0.0)
    tot = jnp.broadcast_to(cum[:, page - 1:page], (n_r, LANES))
    y = _dot(earlier, jnp.concatenate(_split3(tot), axis=1))
    off = y[:, :LANES] + y[:, LANES:2 * LANES] + y[:, 2 * LANES:]
    cum = cum + off + jnp.concatenate([run_ref[...]] * PAGES_PER_STEP, axis=0)
    run = cum[n_r - SUBLANES:, page - 1:page]
    run_ref[...] = run
    s_parts = []
    for n, kp in enumerate(k_pages):
        cum_n = cum[n * SUBLANES:(n + 1) * SUBLANES]
        s_parts.append(_dot(qbd, kp[...]) - jnp.concatenate([cum_n] * n_new, axis=0))
    alpha, p = _online_merge(jnp.concatenate(s_parts, axis=1), m_ref, l_ref)
    acc = alpha * acc_ref[...]
    for n, vp in enumerate(v_pages):
        acc += _dot_nt(p[:, n * page:(n + 1) * page], vp[...])
    acc_ref[...] = acc

    @pl.when(g == pl.num_programs(1) - 1)
    def _():
        kn = kn_ref[...]
        vn = vn_ref[...]
        lf = _log_sigmoid(cf_ref[...] + bf_ref[...])
        logf_ref[...] = lf
        lane = lax.broadcasted_iota(jnp.int32, (n_rows, LANES), 1)
        pick = lane == h_row
        run_rows = jnp.concatenate([run] * n_new, axis=0)
        cum_new = jnp.zeros((1, LANES), F32)
        s_new = []
        for u in range(n_new):
            cum_new = cum_new + lf[u:u + 1]
            cu = jnp.sum(jnp.where(pick, cum_new, 0.0), axis=1, keepdims=True)
            su = jnp.sum(qbd * kn[u:u + 1], axis=1, keepdims=True) - (run_rows + cu)
            s_new.append(jnp.where(t_row >= u, su, NEG_INF))
        alpha, p = _online_merge(jnp.concatenate(s_new, axis=1), m_ref, l_ref)
        acc = alpha * acc_ref[...]
        for u in range(n_new):
            acc += p[:, u:u + 1] * vn[u:u + 1]
        o = acc / l_ref[...]
        col = lax.broadcasted_iota(jnp.int32, o.shape, 1)
        o = _slot_sum(jnp.where(col // HEAD_DIM == h_row, o, 0.0), n_new)
        y = o * _silu(gate_ref[...][:n_new])
        o_ref[...] = jnp.concatenate([y, jnp.zeros((ROWS_S - n_new, W_C), F32)], axis=0)


def _dec_c(proj3, cache_kt, cache_vt, cache_f, layer, pt_flat, bf_pad, n_new):
    b = proj3.shape[0]
    page = cache_kt.shape[3]
    n_pages = pt_flat.shape[0] // b
    n_steps = n_pages // PAGES_PER_STEP

    def new_blk(col, w):
        return pl.BlockSpec((None, ROWS_S, w), lambda bb, s, pt: (bb, 0, col // w))

    def page_idx(bb, s, pt, n):
        return pt[bb * n_pages + s * PAGES_PER_STEP + n]

    def kv_blk(n):
        return pl.BlockSpec((None, None, W_C, page),
                            lambda bb, s, pt: (layer, page_idx(bb, s, pt, n), 0, 0))

    def f_blk(n):
        return pl.BlockSpec((None, None, SUBLANES, page),
                            lambda bb, s, pt: (layer, page_idx(bb, s, pt, n), 0, 0))

    n_rows = n_new * SUBLANES
    return pl.pallas_call(
        functools.partial(_dec_c_kernel, n_new=n_new, page=page, n_pages=n_pages),
        out_shape=(jax.ShapeDtypeStruct((b, ROWS_S, W_C), F32),
                   jax.ShapeDtypeStruct((b, ROWS_S, LANES), F32)),
        grid_spec=pltpu.PrefetchScalarGridSpec(
            num_scalar_prefetch=1,
            grid=(b, n_steps),
            in_specs=[pl.BlockSpec((1, LANES), lambda bb, s, pt: (0, 0)),
                      new_blk(COL['c_q'], W_C), new_blk(COL['c_k'], W_C), new_blk(COL['c_v'], W_C),
                      new_blk(COL['c_f'], LANES), new_blk(COL['c_g'], W_C)]
                     + [kv_blk(n) for n in range(PAGES_PER_STEP)] * 2
                     + [f_blk(n) for n in range(PAGES_PER_STEP)],
            out_specs=(pl.BlockSpec((None, ROWS_S, W_C), lambda bb, s, pt: (bb, 0, 0)),
                       pl.BlockSpec((None, ROWS_S, LANES), lambda bb, s, pt: (bb, 0, 0))),
            scratch_shapes=[pltpu.VMEM((n_rows, W_C), F32),
                            pltpu.VMEM((n_rows, 1), F32),
                            pltpu.VMEM((n_rows, 1), F32),
                            pltpu.VMEM((n_rows, W_C), F32),
                            pltpu.VMEM((SUBLANES, 1), F32)]),
        compiler_params=_cparams(("parallel", "arbitrary")),
        name="dec_c",
    )(pt_flat, bf_pad, proj3, proj3, proj3, proj3, proj3,
      *([cache_kt] * PAGES_PER_STEP), *([cache_vt] * PAGES_PER_STEP), *([cache_f] * PAGES_PER_STEP))


def _pick_tile(n, prefs):
    for t in prefs:
        if n % t == 0:
            return t
    return n


def kernel(x_prompt, x_sample, cache_a_k, cache_a_v, cache_c_k, cache_c_v, cache_c_logf, cache_mem_k, cache_mem_v, state_ret, page_table, mem_prompt, w_in, b_f, lam_q1, lam_k1, lam_q2, lam_k2, a_subln_g, w_out, w_mem_kv, ln_g, ln_b):
    depth, d_model, _ = w_in.shape
    bp, seq, _ = x_prompt.shape
    bs, n_new, _ = x_sample.shape
    n_phys, page = cache_a_k.shape[1], cache_a_k.shape[2]
    n_mem = mem_prompt.shape[1]
    alpha = (2 * depth) ** 0.25
    assert n_new <= ROWS_S and page_table.shape[1] % PAGES_PER_STEP == 0
    assert page == LANES

    w_perm = _permute_w_in(w_in).astype(BF16)
    ck0 = REF_COL['c_k'][0]
    assert REF_COL['c_v'][0] == ck0 + W_C
    wt_ckv = jnp.swapaxes(w_in[:, :, ck0:ck0 + 2 * W_C], 1, 2).astype(BF16)
    w_out_b = w_out.astype(BF16)
    w_mem_b = w_mem_kv.astype(BF16)
    bf_pad = jnp.pad(b_f, ((0, 0), (0, LANES - H_C)))[:, None, :]
    lamv = jnp.stack([lam_q1, lam_k1, lam_q2, lam_k2], axis=1)
    g_a = a_subln_g[:, None, :]
    ln_g3, ln_b3 = ln_g[:, None, :], ln_b[:, None, :]
    slopes = 2.0 ** (-8.0 * jnp.arange(1, H_A + 1, dtype=F32) / H_A)
    ck_a = cache_a_k.reshape(depth, n_phys, page * H_A, 2 * HEAD_DIM)
    cv_a = cache_a_v.reshape(depth, n_phys, page * H_A, 2 * HEAD_DIM)
    ck_c = jnp.transpose(cache_c_k, (0, 1, 3, 4, 2)).reshape(depth, n_phys, W_C, page)
    cv_c = jnp.transpose(cache_c_v, (0, 1, 3, 4, 2)).reshape(depth, n_phys, W_C, page)
    cf_c = jnp.pad(jnp.transpose(cache_c_logf, (0, 1, 3, 2)),
                   ((0, 0), (0, 0), (0, SUBLANES - H_C), (0, 0)))
    pt_flat = page_table.reshape(-1)
    mem2d = mem_prompt.reshape(bp * n_mem, d_model)
    smk = jnp.transpose(cache_mem_k, (0, 1, 3, 4, 2)).reshape(depth * bs, W_M, n_mem)
    smv = jnp.transpose(cache_mem_v, (0, 1, 3, 4, 2)).reshape(depth * bs, W_M, n_mem)

    chunk_p = RET_CHUNK if seq % RET_CHUNK == 0 else seq
    tables_p = _ret_tables(chunk_p, chunk_p)
    tables_s = _ret_tables(ROWS_S, n_new)
    s0_p = jnp.zeros((bp, H_B // 2, LANES, LANES), F32)

    tq = _pick_tile(seq, (512, 256, 128))
    tm_p = _pick_tile(bp * seq, (1024, 512, 256, 128))
    tn = _pick_tile(PW, (1152, 640, 128))
    tn_main = _pick_tile(PW_MAIN, (1664, 384, 128))
    tm_t = _pick_tile(seq, (1024, 512, 256, 128))
    ret_rows = _pick_tile(seq, (1024, 512, 256, 128))
    tq_m = _pick_tile(seq, (1024, 512, 256, 128))

    xp = x_prompt.reshape(bp * seq, d_model)
    xs = jnp.pad(x_sample, ((0, 0), (0, ROWS_S - n_new), (0, 0))).reshape(bs * ROWS_S, d_model)

    outs_p = [[] for _ in range(8)]
    outs_s = [[] for _ in range(6)]
    for l in range(depth):
        lam_init = 0.8 - 0.6 * math.exp(-0.3 * l)

        pj = _proj(xp, w_perm, l, tm_p, tn_main, n_cols=PW_MAIN).reshape(bp, seq, PW_MAIN)
        kvt = _proj_t(xp.reshape(bp, seq, d_model), wt_ckv, l, tm_t)
        logf_t, cum_t = _logf_cum(pj, bf_pad[l])
        o_a = _attn_a(pj, slopes, lamv[l], g_a[l], lam_init, tq)
        o_c = _attn_c(pj, kvt, cum_t, tq)
        o_b, s_new = _retention(pj, s0_p, tables_p, chunk_p, ret_rows)
        mkv = _proj(mem2d, w_mem_b, l, _pick_tile(bp * n_mem, (512, 256, 128)), W_M).reshape(bp, n_mem, 2 * W_M)
        o_m = _mem_attn(pj, mkv, mkv, 0, 1, tq_m, False)
        xp = _out_ln(o_a.reshape(bp * seq, W_A), o_b.reshape(bp * seq, W_B), o_c.reshape(bp * seq, W_C),
                     o_m.reshape(bp * seq, W_M), w_out_b, l, xp, ln_g3, ln_b3, alpha,
                     _pick_tile(bp * seq, (512, 256, 128)))
        outs_p[0].append(pj[:, :, COL['a_k']:COL['a_k'] + W_A].reshape(bp, seq, H_A, 2 * HEAD_DIM))
        outs_p[1].append(pj[:, :, COL['a_v']:COL['a_v'] + W_A].reshape(bp, seq, H_A, 2 * HEAD_DIM))
        kvt5 = kvt.reshape(bp, 2, H_C, HEAD_DIM, seq)
        outs_p[2].append(jnp.transpose(kvt5[:, 0], (0, 3, 1, 2)))
        outs_p[3].append(jnp.transpose(kvt5[:, 1], (0, 3, 1, 2)))
        outs_p[4].append(jnp.transpose(logf_t[:, :H_C, :], (0, 2, 1)))
        outs_p[5].append(_state_from_blockdiag(s_new))
        outs_p[6].append(mkv[:, :, :W_M].reshape(bp, n_mem, H_M, HEAD_DIM))
        outs_p[7].append(mkv[:, :, W_M:].reshape(bp, n_mem, H_M, HEAD_DIM))

        pjs = _proj(xs, w_perm, l, _pick_tile(bs * ROWS_S, (256, 128, 8)), tn).reshape(bs, ROWS_S, PW)
        o_a = _dec_a(pjs, ck_a, cv_a, l, pt_flat, slopes, lamv[l], g_a[l], lam_init, n_new)
        o_c, logf_s = _dec_c(pjs, ck_c, cv_c, cf_c, l, pt_flat, bf_pad[l], n_new)
        o_b, s_new = _retention(pjs, _state_to_blockdiag(state_ret[l]), tables_s, ROWS_S, ROWS_S)
        o_m = _mem_attn(pjs, smk, smv, 0, 0, ROWS_S, True, mem_off=l * bs)
        xs = _out_ln(o_a.reshape(bs * ROWS_S, W_A), o_b.reshape(bs * ROWS_S, W_B),
                     o_c.reshape(bs * ROWS_S, W_C), o_m.reshape(bs * ROWS_S, W_M), w_out_b, l, xs,
                     ln_g3, ln_b3, alpha, _pick_tile(bs * ROWS_S, (256, 128, 8)))
        outs_s[0].append(pjs[:, :n_new, COL['a_k']:COL['a_k'] + W_A].reshape(bs, n_new, H_A, 2 * HEAD_DIM))
        outs_s[1].append(pjs[:, :n_new, COL['a_v']:COL['a_v'] + W_A].reshape(bs, n_new, H_A, 2 * HEAD_DIM))
        outs_s[2].append(pjs[:, :n_new, COL['c_k']:COL['c_k'] + W_C].reshape(bs, n_new, H_C, HEAD_DIM))
        outs_s[3].append(pjs[:, :n_new, COL['c_v']:COL['c_v'] + W_C].reshape(bs, n_new, H_C, HEAD_DIM))
        outs_s[4].append(logf_s[:, :n_new, :H_C])
        outs_s[5].append(_state_from_blockdiag(s_new))

    y_p = xp.reshape(bp, seq, d_model)
    y_s = xs.reshape(bs, ROWS_S, d_model)[:, :n_new]
    return (y_p, y_s, *(jnp.stack(o) for o in outs_p), *(jnp.stack(o) for o in outs_s))
```

```python
import functools
import math

import jax
import jax.numpy as jnp
import numpy as np
from jax import lax
from jax.experimental import pallas as pl
from jax.experimental.pallas import tpu as pltpu

F32 = jnp.float32
BF16 = jnp.bfloat16

HEAD_DIM = 64
H_A, H_B, H_C, H_M = 4, 6, 6, 4
W_A, W_B, W_C, W_M = H_A * 2 * HEAD_DIM, H_B * HEAD_DIM, H_C * HEAD_DIM, H_M * HEAD_DIM
MIX = W_A + W_B + W_C + W_M
LANES = 128
SUBLANES = 8
RET_CHUNK = 128
LN_EPS = 1e-5
NEG_INF = -1e30
QK_SCALE = HEAD_DIM ** -0.5
VMEM_LIMIT = 48 * 1024 * 1024

_REF_GROUPS = (('a_q', W_A), ('a_k', W_A), ('a_v', W_A), ('a_g', W_A),
               ('b_q', W_B), ('b_k', W_B), ('b_v', W_B), ('b_g', W_B),
               ('c_q', W_C), ('c_k', W_C), ('c_v', W_C), ('c_g', W_C), ('c_f', H_C),
               ('m_q', W_M), ('m_g', W_M))
_GROUPS = (('a_q', W_A), ('a_k', W_A), ('a_v', W_A), ('a_g', W_A), ('m_q', W_M),
           ('c_q', W_C), ('c_g', W_C), ('b_q', W_B), ('b_k', W_B), ('b_v', W_B), ('b_g', W_B),
           ('m_g', W_M), ('c_f', LANES), ('c_k', W_C), ('c_v', W_C))
REF_COL, COL = {}, {}
_o = 0
for _n, _w in _REF_GROUPS:
    REF_COL[_n] = (_o, _w)
    _o += _w
_o = 0
for _n, _w in _GROUPS:
    assert _o % _w == 0 and _o % LANES == 0
    COL[_n] = _o
    _o += _w
PW = _o
PW_MAIN = COL['c_k']
BLK = {n: o // LANES for n, o in COL.items()}
LOG2E = math.log2(math.e)


def _permute_w_in(w_in):
    parts = []
    for n, w in _GROUPS:
        o, wr = REF_COL[n]
        parts.append(w_in[:, :, o:o + wr])
        if w > wr:
            parts.append(jnp.zeros(w_in.shape[:2] + (w - wr,), w_in.dtype))
    return jnp.concatenate(parts, axis=-1)


def _cparams(sem):
    return pltpu.CompilerParams(dimension_semantics=sem, vmem_limit_bytes=VMEM_LIMIT)


def _silu(x):
    return x / (1.0 + jnp.exp(-x))


def _dot(a, b):
    return jnp.dot(a.astype(BF16), b.astype(BF16), preferred_element_type=F32)


def _dot_nt(a, b):
    return lax.dot_general(a.astype(BF16), b.astype(BF16), (((1,), (1,)), ((), ())),
                           preferred_element_type=F32)


def _lane_half_masks(shape):
    lane = lax.broadcasted_iota(jnp.int32, shape, len(shape) - 1)
    lo = (lane % LANES) < HEAD_DIM
    return lo, jnp.logical_not(lo)


def _proj_kernel(x_ref, w_ref, o_ref, xb_ref):
    @pl.when(pl.program_id(1) == 0)
    def _():
        xb_ref[...] = x_ref[...].astype(BF16)

    o_ref[...] = jnp.dot(xb_ref[...], w_ref[...], preferred_element_type=F32)


def _proj(x2d, w_all, layer, tm, tn, n_cols=None):
    m, k = x2d.shape
    n = w_all.shape[2] if n_cols is None else n_cols
    return pl.pallas_call(
        _proj_kernel,
        out_shape=jax.ShapeDtypeStruct((m, n), F32),
        grid=(m // tm, n // tn),
        in_specs=[pl.BlockSpec((tm, k), lambda i, j: (i, 0)),
                  pl.BlockSpec((None, k, tn), lambda i, j: (layer, 0, j))],
        out_specs=pl.BlockSpec((tm, tn), lambda i, j: (i, j)),
        scratch_shapes=[pltpu.VMEM((tm, k), BF16)],
        compiler_params=_cparams(("parallel", "arbitrary")),
        name="proj",
    )(x2d, w_all)


def _proj_t_kernel(wt_ref, x_ref, o_ref):
    o_ref[...] = _dot_nt(wt_ref[...], x_ref[...])


def _proj_t(x3d, wt_all, layer, tm):
    b, seq, k = x3d.shape
    n = wt_all.shape[1]
    return pl.pallas_call(
        _proj_t_kernel,
        out_shape=jax.ShapeDtypeStruct((b, n, seq), F32),
        grid=(b, seq // tm),
        in_specs=[pl.BlockSpec((None, n, k), lambda bb, i: (layer, 0, 0)),
                  pl.BlockSpec((None, tm, k), lambda bb, i: (bb, i, 0))],
        out_specs=pl.BlockSpec((None, n, tm), lambda bb, i: (bb, 0, i)),
        compiler_params=_cparams(("parallel", "parallel")),
        name="proj_t",
    )(wt_all, x3d)


def _log_sigmoid(z):
    return jnp.minimum(z, 0.0) - jnp.log1p(jnp.exp(-jnp.abs(z)))


def _lane_cumsum(x):
    n = x.shape[-1]
    lane = lax.broadcasted_iota(jnp.int32, x.shape, x.ndim - 1)
    shift = 1
    while shift < n:
        x = x + jnp.where(lane >= shift, pltpu.roll(x, shift, x.ndim - 1), 0.0)
        shift *= 2
    return x


def _logf_cum_kernel(cf_ref, bf_ref, logf_ref, cum_ref):
    lf = _log_sigmoid(cf_ref[...] + bf_ref[...])
    lf_t = lf.T[:SUBLANES, :]
    logf_ref[...] = lf_t
    cum_ref[...] = _lane_cumsum(lf_t)


def _logf_cum(proj3, bf_pad):
    b, seq, _ = proj3.shape
    out = jax.ShapeDtypeStruct((b, SUBLANES, seq), F32)
    return pl.pallas_call(
        _logf_cum_kernel,
        out_shape=(out, out),
        grid=(b,),
        in_specs=[pl.BlockSpec((None, seq, LANES), lambda i: (i, 0, BLK['c_f'])),
                  pl.BlockSpec((1, LANES), lambda i: (0, 0))],
        out_specs=(pl.BlockSpec((None, SUBLANES, seq), lambda i: (i, 0, 0)),
                   pl.BlockSpec((None, SUBLANES, seq), lambda i: (i, 0, 0))),
        compiler_params=_cparams(("parallel",)),
        name="logf_cum",
    )(proj3, bf_pad)


def _tri_pairs(n_q, ratio):
    qi = np.array([i for i in range(n_q) for _ in range((i + 1) * ratio)], np.int32)
    kj = np.array([j for i in range(n_q) for j in range((i + 1) * ratio)], np.int32)
    return jnp.asarray(qi), jnp.asarray(kj)


SOFTMAX_ROWS = 32


def _stack_queries(q_ref, q2_ref, tq):
    q = q_ref[...] * (QK_SCALE * LOG2E)
    lo, hi = _lane_half_masks(q.shape)
    q2_ref[0:tq, :] = jnp.where(lo, q, 0.0).astype(BF16)
    q2_ref[tq:, :] = jnp.where(hi, q, 0.0).astype(BF16)


def _flash_init(m_ref, l_ref, acc_ref):
    m_ref[...] = jnp.full(m_ref.shape, NEG_INF, F32)
    l_ref[...] = jnp.zeros(l_ref.shape, F32)
    acc_ref[...] = jnp.zeros(acc_ref.shape, F32)


def _softmax_tile(s_ref, p_ref, m_ref, l_ref, alpha_ref, col_bias, tq, q_minus_k):
    tk = s_ref.shape[1]
    rc = SOFTMAX_ROWS
    reps = tk // LANES
    causal = q_minus_k is not None
    if causal:
        col = lax.broadcasted_iota(jnp.int32, (rc, tk), 1)
        row0 = lax.broadcasted_iota(jnp.int32, (rc, tk), 0) + q_minus_k

    def scores(c):
        s = s_ref[c * rc:(c + 1) * rc, :] + col_bias(c * rc // tq)
        if causal:
            s = jnp.where(col <= row0 + (c * rc) % tq, s, NEG_INF)
        return s

    for c in range(2 * tq // rc):
        rows = slice(c * rc, (c + 1) * rc)
        m_old = m_ref[rows, :]
        m_new = jnp.maximum(m_old, jnp.max(scores(c), axis=1, keepdims=True))
        alpha_ref[rows, :] = jnp.exp2(m_old - m_new)
        m_ref[rows, :] = m_new
    for c in range(2 * tq // rc):
        rows = slice(c * rc, (c + 1) * rc)
        p = jnp.exp2(scores(c) - jnp.concatenate([m_ref[rows, :]] * reps, axis=1))
        l_ref[rows, :] = alpha_ref[rows, :] * l_ref[rows, :] + jnp.sum(p, axis=1, keepdims=True)
        p_ref[rows, :] = p.astype(BF16)


def _flash_steps(i, j, tq, tk, init, step, finish):
    ratio = tq // tk

    @pl.when(j == 0)
    def _():
        init()

    @pl.when(j < i * ratio)
    def _():
        step(None)

    @pl.when(j >= i * ratio)
    def _():
        step(i * tq - j * tk)

    @pl.when(j == (i + 1) * ratio - 1)
    def _():
        finish()


def _lambda(lamv_ref, lam_init):
    lv = lamv_ref[...]
    e1 = jnp.exp(jnp.sum(lv[0:1] * lv[1:2], axis=1, keepdims=True))
    e2 = jnp.exp(jnp.sum(lv[2:3] * lv[3:4], axis=1, keepdims=True))
    return e1 - e2 + lam_init


def _attn_a_kernel(qi_ref, kj_ref, slopes_ref, lamv_ref, g_ref, q_ref, k_ref, v_ref, gate_ref,
                   o_ref, q2_ref, s_ref, p_ref, m_ref, l_ref, alpha_ref, acc_ref, *, tq, tk, lam_init):
    h = pl.program_id(1)
    p = pl.program_id(2)
    i = qi_ref[p]
    j = kj_ref[p]

    def init():
        _stack_queries(q_ref, q2_ref, tq)
        _flash_init(m_ref, l_ref, acc_ref)

    def step(q_minus_k):
        s_ref[...] = _dot_nt(q2_ref[...], k_ref[...])
        kpos = lax.broadcasted_iota(jnp.int32, (1, tk), 1) + j * tk
        bias = (slopes_ref[h] * LOG2E) * kpos.astype(F32)
        _softmax_tile(s_ref, p_ref, m_ref, l_ref, alpha_ref, lambda half: bias, tq, q_minus_k)
        acc_ref[...] = alpha_ref[...] * acc_ref[...] + _dot(p_ref[...], v_ref[...])

    def finish():
        o = acc_ref[...] / l_ref[...]
        o = o[:tq] - _lambda(lamv_ref, lam_init) * o[tq:]
        y = o * lax.rsqrt(jnp.mean(o * o, axis=1, keepdims=True) + LN_EPS) * g_ref[...]
        o_ref[...] = y * (1.0 - lam_init) * _silu(gate_ref[...])

    _flash_steps(i, j, tq, tk, init, step, finish)


def _attn_a(proj3, slopes, lamv, g, lam_init, tq, tk):
    b, seq, _ = proj3.shape
    qi, kj = _tri_pairs(seq // tq, tq // tk)

    def blk(col, rows, tile):
        return pl.BlockSpec((None, rows, LANES),
                            lambda bb, h, p, qi, kj: (bb, (qi if tile == 'q' else kj)[p], col + h))

    kern = functools.partial(_attn_a_kernel, tq=tq, tk=tk, lam_init=lam_init)
    return pl.pallas_call(
        kern,
        out_shape=jax.ShapeDtypeStruct((b, seq, W_A), F32),
        grid_spec=pltpu.PrefetchScalarGridSpec(
            num_scalar_prefetch=2,
            grid=(b, H_A, qi.shape[0]),
            in_specs=[pl.BlockSpec(memory_space=pltpu.SMEM),
                      pl.BlockSpec((4, HEAD_DIM), lambda bb, h, p, qi, kj: (0, 0)),
                      pl.BlockSpec((1, LANES), lambda bb, h, p, qi, kj: (0, 0)),
                      blk(BLK['a_q'], tq, 'q'), blk(BLK['a_k'], tk, 'k'), blk(BLK['a_v'], tk, 'k'),
                      blk(BLK['a_g'], tq, 'q')],
            out_specs=pl.BlockSpec((None, tq, LANES), lambda bb, h, p, qi, kj: (bb, qi[p], h)),
            scratch_shapes=_flash_scratch(tq, tk)),
        compiler_params=_cparams(("parallel", "parallel", "arbitrary")),
        name="attn_a",
    )(qi, kj, slopes, lamv, g, proj3, proj3, proj3, proj3)


def _flash_scratch(tq, tk):
    rep = pltpu.VMEM((2 * tq, LANES), F32)
    return [pltpu.VMEM((2 * tq, LANES), BF16),
            pltpu.VMEM((2 * tq, tk), F32),
            pltpu.VMEM((2 * tq, tk), BF16),
            rep, rep, rep, rep]


def _attn_c_kernel(qi_ref, kj_ref, q_ref, kt_ref, vt_ref, cum_ref, gate_ref,
                   o_ref, q2_ref, s_ref, p_ref, m_ref, l_ref, alpha_ref, acc_ref, *, tq, tk):
    hb = pl.program_id(1)
    p = pl.program_id(2)
    i = qi_ref[p]
    j = kj_ref[p]

    def init():
        _stack_queries(q_ref, q2_ref, tq)
        _flash_init(m_ref, l_ref, acc_ref)

    def step(q_minus_k):
        s_ref[...] = _dot(q2_ref[...], kt_ref[...])
        bias = [cum_ref[pl.ds(2 * hb + half, 1), :] * (-LOG2E) for half in range(2)]
        _softmax_tile(s_ref, p_ref, m_ref, l_ref, alpha_ref, lambda half: bias[half], tq, q_minus_k)
        acc_ref[...] = alpha_ref[...] * acc_ref[...] + _dot_nt(p_ref[...], vt_ref[...])

    def finish():
        o = acc_ref[...] / l_ref[...]
        lo, _ = _lane_half_masks((tq, LANES))
        o_ref[...] = jnp.where(lo, o[:tq], o[tq:]) * _silu(gate_ref[...])

    _flash_steps(i, j, tq, tk, init, step, finish)


def _attn_c(proj3, kvt, cum_t, tq, tk):
    b, seq, _ = proj3.shape
    qi, kj = _tri_pairs(seq // tq, tq // tk)
    n_pairs = H_C // 2

    def blk(col):
        return pl.BlockSpec((None, tq, LANES), lambda bb, h, p, qi, kj: (bb, qi[p], col + h))

    def blk_t(row):
        return pl.BlockSpec((None, LANES, tk), lambda bb, h, p, qi, kj: (bb, row + h, kj[p]))

    return pl.pallas_call(
        functools.partial(_attn_c_kernel, tq=tq, tk=tk),
        out_shape=jax.ShapeDtypeStruct((b, seq, W_C), F32),
        grid_spec=pltpu.PrefetchScalarGridSpec(
            num_scalar_prefetch=2,
            grid=(b, n_pairs, qi.shape[0]),
            in_specs=[blk(BLK['c_q']), blk_t(0), blk_t(n_pairs),
                      pl.BlockSpec((None, SUBLANES, tk), lambda bb, h, p, qi, kj: (bb, 0, kj[p])),
                      blk(BLK['c_g'])],
            out_specs=pl.BlockSpec((None, tq, LANES), lambda bb, h, p, qi, kj: (bb, qi[p], h)),
            scratch_shapes=_flash_scratch(tq, tk)),
        compiler_params=_cparams(("parallel", "parallel", "arbitrary")),
        name="attn_c",
    )(qi, kj, proj3, kvt, kvt, cum_t, proj3)


def _ret_tables(chunk, n_valid):
    log_gamma = jnp.log(1.0 - 2.0 ** (-5.0 - jnp.arange(H_B, dtype=F32)))
    idx = jnp.arange(chunk, dtype=F32)
    valid = idx < n_valid
    rel = idx[:, None] - idx[None, :]
    decay = jnp.where(rel >= 0, jnp.exp(log_gamma[:, None, None] * jnp.maximum(rel, 0.0)), 0.0)
    key_rows = max(chunk, LANES)
    decay = jnp.pad(decay, ((0, 0), (0, 0), (0, key_rows - chunk)))
    k_w = jnp.where(valid, jnp.exp(log_gamma[:, None] * (n_valid - 1 - idx)[None, :]), 0.0)
    q_w = jnp.exp(log_gamma[:, None] * (idx + 1)[None, :])
    g = jnp.exp(log_gamma * n_valid)

    def lanes(t):
        t = t.reshape(H_B // 2, 2, chunk)
        return jnp.repeat(jnp.transpose(t, (0, 2, 1)), HEAD_DIM, axis=2)

    g_rows = jnp.repeat(g.reshape(H_B // 2, 2), HEAD_DIM, axis=1)
    g_tab = jnp.broadcast_to(g_rows[:, :, None], (H_B // 2, LANES, LANES))
    return decay.reshape(H_B // 2, 2, chunk, key_rows), lanes(k_w) * QK_SCALE, lanes(q_w), g_tab


def _retention_kernel(q_ref, k_ref, v_ref, gate_ref, s0_ref, decay_ref, kw_ref, qw_ref, g_ref,
                      o_ref, s_out_ref, s_ref, *, chunk, n_chunks):
    t = pl.program_id(2)

    @pl.when(t == 0)
    def _():
        s_ref[...] = s0_ref[...]

    shape = (chunk, LANES)
    lo, hi = _lane_half_masks(shape)
    r = lax.broadcasted_iota(jnp.int32, (LANES, LANES), 0)
    c = lax.broadcasted_iota(jnp.int32, (LANES, LANES), 1)
    same_head = (r < HEAD_DIM) == (c < HEAD_DIM)

    for n in range(n_chunks):
        rows = slice(n * chunk, (n + 1) * chunk)
        q = q_ref[rows, :]
        k = k_ref[rows, :]
        v = v_ref[rows, :]
        kw = k * kw_ref[...]
        if chunk < LANES:
            pad = jnp.zeros((LANES - chunk, LANES), F32)
            k, v, kw = (jnp.concatenate([a, pad], axis=0) for a in (k, v, kw))
        lo_k, hi_k = _lane_half_masks(v.shape)
        state = s_ref[...]
        o_cross = _dot(q * qw_ref[...], state)
        a_lo = _dot_nt(jnp.where(lo, q, 0.0), k) * (QK_SCALE * decay_ref[0])
        a_hi = _dot_nt(jnp.where(hi, q, 0.0), k) * (QK_SCALE * decay_ref[1])
        o = o_cross + _dot(a_lo, jnp.where(lo_k, v, 0.0)) + _dot(a_hi, jnp.where(hi_k, v, 0.0))
        kv = _dot(kw.T, v)
        s_ref[...] = g_ref[...] * state + jnp.where(same_head, kv, 0.0)
        sq = o * o
        ms_lo = jnp.sum(jnp.where(lo, sq, 0.0), axis=1, keepdims=True) * (1.0 / HEAD_DIM)
        ms_hi = jnp.sum(jnp.where(hi, sq, 0.0), axis=1, keepdims=True) * (1.0 / HEAD_DIM)
        y = o * lax.rsqrt(jnp.where(lo, ms_lo, ms_hi) + LN_EPS)
        o_ref[rows, :] = y * _silu(gate_ref[rows, :])

    @pl.when(t == pl.num_programs(2) - 1)
    def _():
        s_out_ref[...] = s_ref[...]


def _retention(proj3, s0_bd, tables, chunk, rows_per_step):
    b, seq, _ = proj3.shape
    decay, kw, qw, g_tab = tables
    n_pairs = H_B // 2
    n_chunks = rows_per_step // chunk

    def blk(col):
        return pl.BlockSpec((None, rows_per_step, LANES), lambda bb, h, t: (bb, t, col + h))

    def tab(shape):
        return pl.BlockSpec((None,) + shape, lambda bb, h, t: (h,) + (0,) * len(shape))

    return pl.pallas_call(
        functools.partial(_retention_kernel, chunk=chunk, n_chunks=n_chunks),
        out_shape=(jax.ShapeDtypeStruct((b, seq, W_B), F32),
                   jax.ShapeDtypeStruct((b, n_pairs, LANES, LANES), F32)),
        grid=(b, n_pairs, seq // rows_per_step),
        in_specs=[blk(BLK['b_q']), blk(BLK['b_k']), blk(BLK['b_v']), blk(BLK['b_g']),
                  pl.BlockSpec((None, None, LANES, LANES), lambda bb, h, t: (bb, h, 0, 0)),
                  tab((2, chunk, max(chunk, LANES))), tab((chunk, LANES)), tab((chunk, LANES)),
                  tab((LANES, LANES))],
        out_specs=(pl.BlockSpec((None, rows_per_step, LANES), lambda bb, h, t: (bb, t, h)),
                   pl.BlockSpec((None, None, LANES, LANES), lambda bb, h, t: (bb, h, 0, 0))),
        scratch_shapes=[pltpu.VMEM((LANES, LANES), F32)],
        compiler_params=_cparams(("parallel", "parallel", "arbitrary")),
        name="retention",
    )(proj3, proj3, proj3, proj3, s0_bd, decay, kw, qw, g_tab)


def _state_to_blockdiag(s):
    b = s.shape[0]
    s = s.reshape(b, H_B // 2, 2, HEAD_DIM, HEAD_DIM)
    z = jnp.zeros_like(s[:, :, 0])
    top = jnp.concatenate([s[:, :, 0], z], axis=-1)
    bot = jnp.concatenate([z, s[:, :, 1]], axis=-1)
    return jnp.concatenate([top, bot], axis=-2)


def _state_from_blockdiag(s):
    b = s.shape[0]
    even = s[:, :, :HEAD_DIM, :HEAD_DIM]
    odd = s[:, :, HEAD_DIM:, HEAD_DIM:]
    return jnp.stack([even, odd], axis=2).reshape(b, H_B, HEAD_DIM, HEAD_DIM)


def _mem_attn_kernel(q_ref, gate_ref, mk_ref, mv_ref, o_ref):
    q = q_ref[...] * QK_SCALE
    lo, hi = _lane_half_masks((q.shape[0], LANES))
    outs = []
    for pair in range(H_M // 2):
        cols = slice(pair * LANES, (pair + 1) * LANES)
        qp = q[:, cols]
        mk = mk_ref[:, cols]
        mv = mv_ref[:, cols]
        o_pair = None
        for mask in (lo, hi):
            s = _dot_nt(jnp.where(mask, qp, 0.0), mk)
            p = jnp.exp(s - jnp.max(s, axis=1, keepdims=True))
            o = _dot(p, mv) / jnp.sum(p, axis=1, keepdims=True)
            o_pair = jnp.where(mask, o, 0.0) if o_pair is None else jnp.where(mask, o, o_pair)
        outs.append(o_pair)
    o_ref[...] = jnp.concatenate(outs, axis=1) * _silu(gate_ref[...])


def _mem_attn_t_kernel(q_ref, gate_ref, mkt_ref, mvt_ref, o_ref):
    q = q_ref[...] * QK_SCALE
    col = lax.broadcasted_iota(jnp.int32, q.shape, 1)
    o_all = None
    for h in range(H_M):
        head = col // HEAD_DIM == h
        s = _dot(jnp.where(head, q, 0.0), mkt_ref[...])
        p = jnp.exp(s - jnp.max(s, axis=1, keepdims=True))
        o = _dot_nt(p, mvt_ref[...]) / jnp.sum(p, axis=1, keepdims=True)
        o_all = jnp.where(head, o, 0.0) if o_all is None else jnp.where(head, o, o_all)
    o_ref[...] = o_all * _silu(gate_ref[...])


def _mem_attn(proj3, mk_arr, mv_arr, mk_blk, mv_blk, tq, transposed, mem_off=0):
    b, seq, _ = proj3.shape
    mem_block = (None, W_M, mk_arr.shape[2]) if transposed else (None, mk_arr.shape[1], W_M)
    return pl.pallas_call(
        _mem_attn_t_kernel if transposed else _mem_attn_kernel,
        out_shape=jax.ShapeDtypeStruct((b, seq, W_M), F32),
        grid=(b, seq // tq),
        in_specs=[pl.BlockSpec((None, tq, W_M), lambda bb, i: (bb, i, COL['m_q'] // W_M)),
                  pl.BlockSpec((None, tq, W_M), lambda bb, i: (bb, i, COL['m_g'] // W_M)),
                  pl.BlockSpec(mem_block, lambda bb, i: (bb + mem_off, 0, mk_blk)),
                  pl.BlockSpec(mem_block, lambda bb, i: (bb + mem_off, 0, mv_blk))],
        out_specs=pl.BlockSpec((None, tq, W_M), lambda bb, i: (bb, i, 0)),
        compiler_params=_cparams(("parallel", "parallel")),
        name="mem_attn",
    )(proj3, proj3, mk_arr, mv_arr)


def _out_ln_kernel(oa_ref, ob_ref, oc_ref, om_ref, w_ref, x_ref, g_ref, b_ref, y_ref, *, alpha):
    y = _dot(oa_ref[...], w_ref[0:W_A, :])
    y += _dot(ob_ref[...], w_ref[W_A:W_A + W_B, :])
    y += _dot(oc_ref[...], w_ref[W_A + W_B:W_A + W_B + W_C, :])
    y += _dot(om_ref[...], w_ref[W_A + W_B + W_C:, :])
    z = alpha * x_ref[...] + y
    mu = jnp.mean(z, axis=1, keepdims=True)
    zc = z - mu
    var = jnp.mean(zc * zc, axis=1, keepdims=True)
    y_ref[...] = zc * lax.rsqrt(var + LN_EPS) * g_ref[...] + b_ref[...]


def _out_ln(o_a, o_b, o_c, o_m, w_out_all, layer, x2d, ln_g, ln_b, alpha, tm):
    m, d = x2d.shape

    def rows(w):
        return pl.BlockSpec((tm, w), lambda i: (i, 0))

    return pl.pallas_call(
        functools.partial(_out_ln_kernel, alpha=alpha),
        out_shape=jax.ShapeDtypeStruct((m, d), F32),
        grid=(m // tm,),
        in_specs=[rows(W_A), rows(W_B), rows(W_C), rows(W_M),
                  pl.BlockSpec((None, MIX, d), lambda i: (layer, 0, 0)),
                  rows(d),
                  pl.BlockSpec((None, 1, d), lambda i: (layer, 0, 0)),
                  pl.BlockSpec((None, 1, d), lambda i: (layer, 0, 0))],
        out_specs=rows(d),
        compiler_params=_cparams(("parallel",)),
        name="out_ln",
    )(o_a, o_b, o_c, o_m, w_out_all, x2d, ln_g, ln_b)


PAGES_PER_STEP = 16
ROWS_S = 8


def _split3(x):
    hi = x.astype(BF16).astype(F32)
    mid = (x - hi).astype(BF16).astype(F32)
    lo = x - hi - mid
    return hi, mid, lo


def _online_merge(s, m_ref, l_ref):
    m_old = m_ref[...]
    m_new = jnp.maximum(m_old, jnp.max(s, axis=1, keepdims=True))
    alpha = jnp.exp(m_old - m_new)
    p = jnp.exp(s - m_new)
    l_ref[...] = alpha * l_ref[...] + jnp.sum(p, axis=1, keepdims=True)
    m_ref[...] = m_new
    return alpha, p


def _rows_from_tokens(x, n_new):
    return jnp.concatenate([jnp.broadcast_to(x[t:t + 1], (SUBLANES, x.shape[1])) for t in range(n_new)],
                           axis=0)


def _slot_sum(x, n_new):
    return jnp.sum(x.reshape(n_new, SUBLANES, x.shape[1]), axis=1)


def _dec_a_kernel(pt_ref, slopes_ref, lamv_ref, g_ref, q_ref, kn_ref, vn_ref, gate_ref, *rest,
                  n_new, n_past, page, lam_init):
    k_pages = rest[:PAGES_PER_STEP]
    v_pages = rest[PAGES_PER_STEP:2 * PAGES_PER_STEP]
    o_ref, q_rows_ref, m_ref, l_ref, acc_ref = rest[2 * PAGES_PER_STEP:]
    g = pl.program_id(1)
    n_rows = n_new * SUBLANES
    row = lax.broadcasted_iota(jnp.int32, (n_rows, 1), 0)
    t_row = row // SUBLANES
    h_row = (row % SUBLANES) // 2
    slope_row = jnp.zeros((n_rows, 1), F32)
    for h in range(H_A):
        slope_row = jnp.where(h_row == h, slopes_ref[h], slope_row)

    def head_rows(x):
        out = jnp.zeros((n_rows, LANES), F32)
        for h in range(H_A):
            out = jnp.where(h_row == h, _rows_from_tokens(x[:, h * LANES:(h + 1) * LANES], n_new), out)
        return out

    @pl.when(g == 0)
    def _():
        q = head_rows(q_ref[...] * QK_SCALE)
        lane = lax.broadcasted_iota(jnp.int32, q.shape, 1)
        q_rows_ref[...] = jnp.where(lane // HEAD_DIM == row % 2, q, 0.0)
        _flash_init(m_ref, l_ref, acc_ref)

    q_rows = q_rows_ref[...]
    col = lax.broadcasted_iota(jnp.int32, (n_rows, page * H_A), 1)
    own = (col % H_A) == h_row
    rel = (col // H_A - (n_past + t_row)).astype(F32)
    bias0 = jnp.where(own, slope_row * rel, NEG_INF)
    s_parts = []
    for n, kp in enumerate(k_pages):
        base = ((g * PAGES_PER_STEP + n) * page).astype(F32)
        s_parts.append(_dot_nt(q_rows, kp[...]) + (bias0 + slope_row * base))
    alpha, p = _online_merge(jnp.concatenate(s_parts, axis=1), m_ref, l_ref)
    acc = alpha * acc_ref[...]
    w = page * H_A
    for n, vp in enumerate(v_pages):
        acc += _dot(p[:, n * w:(n + 1) * w], vp[...])
    acc_ref[...] = acc

    @pl.when(g == pl.num_programs(1) - 1)
    def _():
        kn = kn_ref[...]
        vn = vn_ref[...]
        s_new = []
        for u in range(n_new):
            ku = head_rows(jnp.broadcast_to(kn[u:u + 1], (n_new, W_A)))
            su = jnp.sum(q_rows * ku, axis=1, keepdims=True) + slope_row * (u - t_row).astype(F32)
            s_new.append(jnp.where(t_row >= u, su, NEG_INF))
        alpha, p = _online_merge(jnp.concatenate(s_new, axis=1), m_ref, l_ref)
        acc = alpha * acc_ref[...]
        for u in range(n_new):
            acc += p[:, u:u + 1] * head_rows(jnp.broadcast_to(vn[u:u + 1], (n_new, W_A)))
        o = acc / l_ref[...]
        o = o * jnp.where(row % 2 == 0, 1.0, -_lambda(lamv_ref, lam_init))
        outs = []
        for h in range(H_A):
            oh = _slot_sum(jnp.where(h_row == h, o, 0.0), n_new)
            yh = oh * lax.rsqrt(jnp.mean(oh * oh, axis=1, keepdims=True) + LN_EPS) * g_ref[...]
            outs.append(yh * (1.0 - lam_init))
        y = jnp.concatenate(outs, axis=1) * _silu(gate_ref[...][:n_new])
        o_ref[...] = jnp.concatenate([y, jnp.zeros((ROWS_S - n_new, W_A), F32)], axis=0)


def _dec_a(proj3, cache_k, cache_v, layer, pt_flat, slopes, lamv, g, lam_init, n_new):
    b = proj3.shape[0]
    page = cache_k.shape[2] // H_A
    n_pages = pt_flat.shape[0] // b
    n_steps = n_pages // PAGES_PER_STEP

    def new_blk(col):
        return pl.BlockSpec((None, ROWS_S, W_A), lambda bb, s, pt: (bb, 0, col // W_A))

    def page_blk(n):
        return pl.BlockSpec((None, None, page * H_A, LANES),
                            lambda bb, s, pt: (layer, pt[bb * n_pages + s * PAGES_PER_STEP + n], 0, 0))

    n_rows = n_new * SUBLANES
    kern = functools.partial(_dec_a_kernel, n_new=n_new, n_past=n_pages * page, page=page,
                             lam_init=lam_init)
    return pl.pallas_call(
        kern,
        out_shape=jax.ShapeDtypeStruct((b, ROWS_S, W_A), F32),
        grid_spec=pltpu.PrefetchScalarGridSpec(
            num_scalar_prefetch=1,
            grid=(b, n_steps),
            in_specs=[pl.BlockSpec(memory_space=pltpu.SMEM),
                      pl.BlockSpec((4, HEAD_DIM), lambda bb, s, pt: (0, 0)),
                      pl.BlockSpec((1, LANES), lambda bb, s, pt: (0, 0)),
                      new_blk(COL['a_q']), new_blk(COL['a_k']), new_blk(COL['a_v']),
                      new_blk(COL['a_g'])]
                     + [page_blk(n) for n in range(PAGES_PER_STEP)] * 2,
            out_specs=pl.BlockSpec((None, ROWS_S, W_A), lambda bb, s, pt: (bb, 0, 0)),
            scratch_shapes=[pltpu.VMEM((n_rows, LANES), F32),
                            pltpu.VMEM((n_rows, 1), F32),
                            pltpu.VMEM((n_rows, 1), F32),
                            pltpu.VMEM((n_rows, LANES), F32)]),
        compiler_params=_cparams(("parallel", "arbitrary")),
        name="dec_a",
    )(pt_flat, slopes, lamv, g, proj3, proj3, proj3, proj3,
      *([cache_k] * PAGES_PER_STEP), *([cache_v] * PAGES_PER_STEP))


def _dec_c_kernel(pt_ref, bf_ref, q_ref, kn_ref, vn_ref, cf_ref, gate_ref, *rest,
                  n_new, page, n_pages):
    k_pages = rest[:PAGES_PER_STEP]
    v_pages = rest[PAGES_PER_STEP:2 * PAGES_PER_STEP]
    f_pages = rest[2 * PAGES_PER_STEP:3 * PAGES_PER_STEP]
    o_ref, logf_ref, qbd_ref, m_ref, l_ref, acc_ref, run_ref = rest[3 * PAGES_PER_STEP:]
    b = pl.program_id(0)
    g = pl.program_id(1)
    n_rows = n_new * SUBLANES
    row = lax.broadcasted_iota(jnp.int32, (n_rows, 1), 0)
    t_row = row // SUBLANES
    h_row = row % SUBLANES

    @pl.when(g == 0)
    def _():
        qr = _rows_from_tokens(q_ref[...] * QK_SCALE, n_new)
        col = lax.broadcasted_iota(jnp.int32, (n_rows, W_C), 1)
        qbd_ref[...] = jnp.where(col // HEAD_DIM == h_row, qr, 0.0)
        _flash_init(m_ref, l_ref, acc_ref)
        run_ref[...] = jnp.zeros(run_ref.shape, F32)

    qbd = qbd_ref[...]
    n_r = PAGES_PER_STEP * SUBLANES
    lf = jnp.concatenate([fp[...] for fp in f_pages], axis=0)
    tok_r = lax.broadcasted_iota(jnp.int32, (page, page), 0)
    tok_c = lax.broadcasted_iota(jnp.int32, (page, page), 1)
    y = _dot(jnp.concatenate(_split3(lf), axis=0), jnp.where(tok_r <= tok_c, 1.0, 0.0))
    cum = y[:n_r] + y[n_r:2 * n_r] + y[2 * n_r:]
    r = lax.broadcasted_iota(jnp.int32, (n_r, n_r), 0)
    c = lax.broadcasted_iota(jnp.int32, (n_r, n_r), 1)
    earlier = jnp.where((r % SUBLANES == c % SUBLANES) & (c < r), 1.0, 0.0)
    tot = jnp.broadcast_to(cum[:, page - 1:page], (n_r, LANES))
    y = _dot(earlier, jnp.concatenate(_split3(tot), axis=1))
    off = y[:, :LANES] + y[:, LANES:2 * LANES] + y[:, 2 * LANES:]
    cum = cum + off + jnp.concatenate([run_ref[...]] * PAGES_PER_STEP, axis=0)
    run = cum[n_r - SUBLANES:, page - 1:page]
    run_ref[...] = run
    s_parts = []
    for n, kp in enumerate(k_pages):
        cum_n = cum[n * SUBLANES:(n + 1) * SUBLANES]
        s_parts.append(_dot(qbd, kp[...]) - jnp.concatenate([cum_n] * n_new, axis=0))
    alpha, p = _online_merge(jnp.concatenate(s_parts, axis=1), m_ref, l_ref)
    acc = alpha * acc_ref[...]
    for n, vp in enumerate(v_pages):
        acc += _dot_nt(p[:, n * page:(n + 1) * page], vp[...])
    acc_ref[...] = acc

    @pl.when(g == pl.num_programs(1) - 1)
    def _():
        kn = kn_ref[...]
        vn = vn_ref[...]
        lf = _log_sigmoid(cf_ref[...] + bf_ref[...])
        logf_ref[...] = lf
        lane = lax.broadcasted_iota(jnp.int32, (n_rows, LANES), 1)
        pick = lane == h_row
        run_rows = jnp.concatenate([run] * n_new, axis=0)
        cum_new = jnp.zeros((1, LANES), F32)
        s_new = []
        for u in range(n_new):
            cum_new = cum_new + lf[u:u + 1]
            cu = jnp.sum(jnp.where(pick, cum_new, 0.0), axis=1, keepdims=True)
            su = jnp.sum(qbd * kn[u:u + 1], axis=1, keepdims=True) - (run_rows + cu)
            s_new.append(jnp.where(t_row >= u, su, NEG_INF))
        alpha, p = _online_merge(jnp.concatenate(s_new, axis=1), m_ref, l_ref)
        acc = alpha * acc_ref[...]
        for u in range(n_new):
            acc += p[:, u:u + 1] * vn[u:u + 1]
        o = acc / l_ref[...]
        col = lax.broadcasted_iota(jnp.int32, o.shape, 1)
        o = _slot_sum(jnp.where(col // HEAD_DIM == h_row, o, 0.0), n_new)
        y = o * _silu(gate_ref[...][:n_new])
        o_ref[...] = jnp.concatenate([y, jnp.zeros((ROWS_S - n_new, W_C), F32)], axis=0)


def _dec_c(proj3, cache_kt, cache_vt, cache_f, layer, pt_flat, bf_pad, n_new):
    b = proj3.shape[0]
    page = cache_kt.shape[3]
    n_pages = pt_flat.shape[0] // b
    n_steps = n_pages // PAGES_PER_STEP

    def new_blk(col, w):
        return pl.BlockSpec((None, ROWS_S, w), lambda bb, s, pt: (bb, 0, col // w))

    def page_idx(bb, s, pt, n):
        return pt[bb * n_pages + s * PAGES_PER_STEP + n]

    def kv_blk(n):
        return pl.BlockSpec((None, None, W_C, page),
                            lambda bb, s, pt: (layer, page_idx(bb, s, pt, n), 0, 0))

    def f_blk(n):
        return pl.BlockSpec((None, None, SUBLANES, page),
                            lambda bb, s, pt: (layer, page_idx(bb, s, pt, n), 0, 0))

    n_rows = n_new * SUBLANES
    return pl.pallas_call(
        functools.partial(_dec_c_kernel, n_new=n_new, page=page, n_pages=n_pages),
        out_shape=(jax.ShapeDtypeStruct((b, ROWS_S, W_C), F32),
                   jax.ShapeDtypeStruct((b, ROWS_S, LANES), F32)),
        grid_spec=pltpu.PrefetchScalarGridSpec(
            num_scalar_prefetch=1,
            grid=(b, n_steps),
            in_specs=[pl.BlockSpec((1, LANES), lambda bb, s, pt: (0, 0)),
                      new_blk(COL['c_q'], W_C), new_blk(COL['c_k'], W_C), new_blk(COL['c_v'], W_C),
                      new_blk(COL['c_f'], LANES), new_blk(COL['c_g'], W_C)]
                     + [kv_blk(n) for n in range(PAGES_PER_STEP)] * 2
                     + [f_blk(n) for n in range(PAGES_PER_STEP)],
            out_specs=(pl.BlockSpec((None, ROWS_S, W_C), lambda bb, s, pt: (bb, 0, 0)),
                       pl.BlockSpec((None, ROWS_S, LANES), lambda bb, s, pt: (bb, 0, 0))),
            scratch_shapes=[pltpu.VMEM((n_rows, W_C), F32),
                            pltpu.VMEM((n_rows, 1), F32),
                            pltpu.VMEM((n_rows, 1), F32),
                            pltpu.VMEM((n_rows, W_C), F32),
                            pltpu.VMEM((SUBLANES, 1), F32)]),
        compiler_params=_cparams(("parallel", "arbitrary")),
        name="dec_c",
    )(pt_flat, bf_pad, proj3, proj3, proj3, proj3, proj3,
      *([cache_kt] * PAGES_PER_STEP), *([cache_vt] * PAGES_PER_STEP), *([cache_f] * PAGES_PER_STEP))


def _pick_tile(n, prefs):
    for t in prefs:
        if n % t == 0:
            return t
    return n


def kernel(x_prompt, x_sample, cache_a_k, cache_a_v, cache_c_k, cache_c_v, cache_c_logf, cache_mem_k, cache_mem_v, state_ret, page_table, mem_prompt, w_in, b_f, lam_q1, lam_k1, lam_q2, lam_k2, a_subln_g, w_out, w_mem_kv, ln_g, ln_b):
    depth, d_model, _ = w_in.shape
    bp, seq, _ = x_prompt.shape
    bs, n_new, _ = x_sample.shape
    n_phys, page = cache_a_k.shape[1], cache_a_k.shape[2]
    n_mem = mem_prompt.shape[1]
    alpha = (2 * depth) ** 0.25
    assert n_new <= ROWS_S and page_table.shape[1] % PAGES_PER_STEP == 0
    assert page == LANES

    w_perm = _permute_w_in(w_in).astype(BF16)
    ck0 = REF_COL['c_k'][0]
    assert REF_COL['c_v'][0] == ck0 + W_C
    wt_ckv = jnp.swapaxes(w_in[:, :, ck0:ck0 + 2 * W_C], 1, 2).astype(BF16)
    w_out_b = w_out.astype(BF16)
    w_mem_b = w_mem_kv.astype(BF16)
    bf_pad = jnp.pad(b_f, ((0, 0), (0, LANES - H_C)))[:, None, :]
    lamv = jnp.stack([lam_q1, lam_k1, lam_q2, lam_k2], axis=1)
    g_a = a_subln_g[:, None, :]
    ln_g3, ln_b3 = ln_g[:, None, :], ln_b[:, None, :]
    slopes = 2.0 ** (-8.0 * jnp.arange(1, H_A + 1, dtype=F32) / H_A)
    ck_a = cache_a_k.reshape(depth, n_phys, page * H_A, 2 * HEAD_DIM)
    cv_a = cache_a_v.reshape(depth, n_phys, page * H_A, 2 * HEAD_DIM)
    ck_c = jnp.transpose(cache_c_k, (0, 1, 3, 4, 2)).reshape(depth, n_phys, W_C, page)
    cv_c = jnp.transpose(cache_c_v, (0, 1, 3, 4, 2)).reshape(depth, n_phys, W_C, page)
    cf_c = jnp.pad(jnp.transpose(cache_c_logf, (0, 1, 3, 2)),
                   ((0, 0), (0, 0), (0, SUBLANES - H_C), (0, 0)))
    pt_flat = page_table.reshape(-1)
    mem2d = mem_prompt.reshape(bp * n_mem, d_model)
    smk = jnp.transpose(cache_mem_k, (0, 1, 3, 4, 2)).reshape(depth * bs, W_M, n_mem)
    smv = jnp.transpose(cache_mem_v, (0, 1, 3, 4, 2)).reshape(depth * bs, W_M, n_mem)

    chunk_p = RET_CHUNK if seq % RET_CHUNK == 0 else seq
    tables_p = _ret_tables(chunk_p, chunk_p)
    tables_s = _ret_tables(ROWS_S, n_new)
    s0_p = jnp.zeros((bp, H_B // 2, LANES, LANES), F32)

    tq = _pick_tile(seq, (1024, 512, 256, 128))
    tk = _pick_tile(tq, (512, 256, 128))
    tm_p = _pick_tile(bp * seq, (1024, 512, 256, 128))
    tn = _pick_tile(PW, (1152, 640, 128))
    tn_main = _pick_tile(PW_MAIN, (1664, 384, 128))
    tm_t = _pick_tile(seq, (1024, 512, 256, 128))
    ret_rows = _pick_tile(seq, (1024, 512, 256, 128))
    tq_m = _pick_tile(seq, (1024, 512, 256, 128))

    xp = x_prompt.reshape(bp * seq, d_model)
    xs = jnp.pad(x_sample, ((0, 0), (0, ROWS_S - n_new), (0, 0))).reshape(bs * ROWS_S, d_model)

    outs_p = [[] for _ in range(8)]
    outs_s = [[] for _ in range(6)]
    for l in range(depth):
        lam_init = 0.8 - 0.6 * math.exp(-0.3 * l)

        pj = _proj(xp, w_perm, l, tm_p, tn_main, n_cols=PW_MAIN).reshape(bp, seq, PW_MAIN)
        kvt = _proj_t(xp.reshape(bp, seq, d_model), wt_ckv, l, tm_t)
        logf_t, cum_t = _logf_cum(pj, bf_pad[l])
        o_a = _attn_a(pj, slopes, lamv[l], g_a[l], lam_init, tq, tk)
        o_c = _attn_c(pj, kvt, cum_t, tq, tk)
        o_b, s_new = _retention(pj, s0_p, tables_p, chunk_p, ret_rows)
        mkv = _proj(mem2d, w_mem_b, l, _pick_tile(bp * n_mem, (512, 256, 128)), W_M).reshape(bp, n_mem, 2 * W_M)
        o_m = _mem_attn(pj, mkv, mkv, 0, 1, tq_m, False)
        xp = _out_ln(o_a.reshape(bp * seq, W_A), o_b.reshape(bp * seq, W_B), o_c.reshape(bp * seq, W_C),
                     o_m.reshape(bp * seq, W_M), w_out_b, l, xp, ln_g3, ln_b3, alpha,
                     _pick_tile(bp * seq, (512, 256, 128)))
        outs_p[0].append(pj[:, :, COL['a_k']:COL['a_k'] + W_A].reshape(bp, seq, H_A, 2 * HEAD_DIM))
        outs_p[1].append(pj[:, :, COL['a_v']:COL['a_v'] + W_A].reshape(bp, seq, H_A, 2 * HEAD_DIM))
        kvt5 = kvt.reshape(bp, 2, H_C, HEAD_DIM, seq)
        outs_p[2].append(jnp.transpose(kvt5[:, 0], (0, 3, 1, 2)))
        outs_p[3].append(jnp.transpose(kvt5[:, 1], (0, 3, 1, 2)))
        outs_p[4].append(jnp.transpose(logf_t[:, :H_C, :], (0, 2, 1)))
        outs_p[5].append(_state_from_blockdiag(s_new))
        outs_p[6].append(mkv[:, :, :W_M].reshape(bp, n_mem, H_M, HEAD_DIM))
        outs_p[7].append(mkv[:, :, W_M:].reshape(bp, n_mem, H_M, HEAD_DIM))

        pjs = _proj(xs, w_perm, l, _pick_tile(bs * ROWS_S, (256, 128, 8)), tn).reshape(bs, ROWS_S, PW)
        o_a = _dec_a(pjs, ck_a, cv_a, l, pt_flat, slopes, lamv[l], g_a[l], lam_init, n_new)
        o_c, logf_s = _dec_c(pjs, ck_c, cv_c, cf_c, l, pt_flat, bf_pad[l], n_new)
        o_b, s_new = _retention(pjs, _state_to_blockdiag(state_ret[l]), tables_s, ROWS_S, ROWS_S)
        o_m = _mem_attn(pjs, smk, smv, 0, 0, ROWS_S, True, mem_off=l * bs)
        xs = _out_ln(o_a.reshape(bs * ROWS_S, W_A), o_b.reshape(bs * ROWS_S, W_B),
                     o_c.reshape(bs * ROWS_S, W_C), o_m.reshape(bs * ROWS_S, W_M), w_out_b, l, xs,
                     ln_g3, ln_b3, alpha, _pick_tile(bs * ROWS_S, (256, 128, 8)))
        outs_s[0].append(pjs[:, :n_new, COL['a_k']:COL['a_k'] + W_A].reshape(bs, n_new, H_A, 2 * HEAD_DIM))
        outs_s[1].append(pjs[:, :n_new, COL['a_v']:COL['a_v'] + W_A].reshape(bs, n_new, H_A, 2 * HEAD_DIM))
        outs_s[2].append(pjs[:, :n_new, COL['c_k']:COL['c_k'] + W_C].reshape(bs, n_new, H_C, HEAD_DIM))
        outs_s[3].append(pjs[:, :n_new, COL['c_v']:COL['c_v'] + W_C].reshape(bs, n_new, H_C, HEAD_DIM))
        outs_s[4].append(logf_s[:, :n_new, :H_C])
        outs_s[5].append(_state_from_blockdiag(s_new))

    y_p = xp.reshape(bp, seq, d_model)
    y_s = xs.reshape(bs, ROWS_S, d_model)[:, :n_new]
    return (y_p, y_s, *(jnp.stack(o) for o in outs_p), *(jnp.stack(o) for o in outs_s))
```

```python
import functools
import math

import jax
import jax.numpy as jnp
import numpy as np
from jax import lax
from jax.experimental import pallas as pl
from jax.experimental.pallas import tpu as pltpu

F32 = jnp.float32
BF16 = jnp.bfloat16

HEAD_DIM = 64
H_A, H_B, H_C, H_M = 4, 6, 6, 4
W_A, W_B, W_C, W_M = H_A * 2 * HEAD_DIM, H_B * HEAD_DIM, H_C * HEAD_DIM, H_M * HEAD_DIM
MIX = W_A + W_B + W_C + W_M
LANES = 128
SUBLANES = 8
RET_CHUNK = 128
LN_EPS = 1e-5
NEG_INF = -1e30
QK_SCALE = HEAD_DIM ** -0.5
VMEM_LIMIT = 48 * 1024 * 1024

_REF_GROUPS = (('a_q', W_A), ('a_k', W_A), ('a_v', W_A), ('a_g', W_A),
               ('b_q', W_B), ('b_k', W_B), ('b_v', W_B), ('b_g', W_B),
               ('c_q', W_C), ('c_k', W_C), ('c_v', W_C), ('c_g', W_C), ('c_f', H_C),
               ('m_q', W_M), ('m_g', W_M))
_GROUPS = (('a_q', W_A), ('a_k', W_A), ('a_v', W_A), ('a_g', W_A), ('m_q', W_M),
           ('c_q', W_C), ('c_g', W_C), ('b_q', W_B), ('b_k', W_B), ('b_v', W_B), ('b_g', W_B),
           ('m_g', W_M), ('c_f', LANES), ('c_k', W_C), ('c_v', W_C))
REF_COL, COL = {}, {}
_o = 0
for _n, _w in _REF_GROUPS:
    REF_COL[_n] = (_o, _w)
    _o += _w
_o = 0
for _n, _w in _GROUPS:
    assert _o % _w == 0 and _o % LANES == 0
    COL[_n] = _o
    _o += _w
PW = _o
PW_MAIN = COL['c_k']
BLK = {n: o // LANES for n, o in COL.items()}
LOG2E = math.log2(math.e)


def _permute_w_in(w_in):
    parts = []
    for n, w in _GROUPS:
        o, wr = REF_COL[n]
        parts.append(w_in[:, :, o:o + wr])
        if w > wr:
            parts.append(jnp.zeros(w_in.shape[:2] + (w - wr,), w_in.dtype))
    return jnp.concatenate(parts, axis=-1)


def _cparams(sem):
    return pltpu.CompilerParams(dimension_semantics=sem, vmem_limit_bytes=VMEM_LIMIT)


def _silu(x):
    return x / (1.0 + jnp.exp(-x))


def _dot(a, b):
    return jnp.dot(a.astype(BF16), b.astype(BF16), preferred_element_type=F32)


def _dot_nt(a, b):
    return lax.dot_general(a.astype(BF16), b.astype(BF16), (((1,), (1,)), ((), ())),
                           preferred_element_type=F32)


def _lane_half_masks(shape):
    lane = lax.broadcasted_iota(jnp.int32, shape, len(shape) - 1)
    lo = (lane % LANES) < HEAD_DIM
    return lo, jnp.logical_not(lo)


def _proj_kernel(x_ref, w_ref, o_ref, xb_ref):
    @pl.when(pl.program_id(1) == 0)
    def _():
        xb_ref[...] = x_ref[...].astype(BF16)

    o_ref[...] = jnp.dot(xb_ref[...], w_ref[...], preferred_element_type=F32)


def _proj(x2d, w_all, layer, tm, tn, n_cols=None):
    m, k = x2d.shape
    n = w_all.shape[2] if n_cols is None else n_cols
    return pl.pallas_call(
        _proj_kernel,
        out_shape=jax.ShapeDtypeStruct((m, n), F32),
        grid=(m // tm, n // tn),
        in_specs=[pl.BlockSpec((tm, k), lambda i, j: (i, 0)),
                  pl.BlockSpec((None, k, tn), lambda i, j: (layer, 0, j))],
        out_specs=pl.BlockSpec((tm, tn), lambda i, j: (i, j)),
        scratch_shapes=[pltpu.VMEM((tm, k), BF16)],
        compiler_params=_cparams(("parallel", "arbitrary")),
        name="proj",
    )(x2d, w_all)


def _proj_main_kernel(x_ref, w_ref, *rest):
    o_ref, ak_ref, av_ref, xb_ref = rest[-4:]
    first = pl.program_id(1) == 0

    @pl.when(first)
    def _():
        xb_ref[...] = x_ref[...].astype(BF16)

    o = jnp.dot(xb_ref[...], w_ref[...], preferred_element_type=F32)
    o_ref[...] = o

    @pl.when(first)
    def _():
        ak_ref[...] = o[:, COL['a_k']:COL['a_k'] + W_A]
        av_ref[...] = o[:, COL['a_v']:COL['a_v'] + W_A]


def _proj_main(x2d, w_all, layer, tm, tn, stacks):
    m, k = x2d.shape
    depth = w_all.shape[0]
    assert tn >= COL['a_v'] + W_A and COL['a_k'] < COL['a_v']
    stack = jax.ShapeDtypeStruct((depth, m, W_A), F32)
    stack_spec = pl.BlockSpec((None, tm, W_A), lambda i, j: (layer, i, 0))
    return pl.pallas_call(
        _proj_main_kernel,
        out_shape=(jax.ShapeDtypeStruct((m, PW_MAIN), F32), stack, stack),
        grid=(m // tm, PW_MAIN // tn),
        in_specs=[pl.BlockSpec((tm, k), lambda i, j: (i, 0)),
                  pl.BlockSpec((None, k, tn), lambda i, j: (layer, 0, j))]
                 + [pl.BlockSpec(memory_space=pl.ANY)] * len(stacks),
        out_specs=(pl.BlockSpec((tm, tn), lambda i, j: (i, j)), stack_spec, stack_spec),
        input_output_aliases={2: 1, 3: 2} if stacks else {},
        scratch_shapes=[pltpu.VMEM((tm, k), BF16)],
        compiler_params=_cparams(("parallel", "arbitrary")),
        name="proj_main",
    )(x2d, w_all, *stacks)


def _proj_t_kernel(wt_ref, x_ref, *rest):
    ck_ref, cv_ref = rest[-2:]
    o = _dot_nt(wt_ref[...], x_ref[...])
    ck_ref[...] = o[:W_C]
    cv_ref[...] = o[W_C:]


def _proj_t(x3d, wt_all, layer, tm, stacks):
    b, seq, k = x3d.shape
    depth = wt_all.shape[0]
    stack = jax.ShapeDtypeStruct((depth, b, W_C, seq), F32)
    stack_spec = pl.BlockSpec((None, None, W_C, tm), lambda bb, i: (layer, bb, 0, i))
    return pl.pallas_call(
        _proj_t_kernel,
        out_shape=(stack, stack),
        grid=(b, seq // tm),
        in_specs=[pl.BlockSpec((None, 2 * W_C, k), lambda bb, i: (layer, 0, 0)),
                  pl.BlockSpec((None, tm, k), lambda bb, i: (bb, i, 0))]
                 + [pl.BlockSpec(memory_space=pl.ANY)] * len(stacks),
        out_specs=(stack_spec, stack_spec),
        input_output_aliases={2: 0, 3: 1} if stacks else {},
        compiler_params=_cparams(("parallel", "parallel")),
        name="proj_t",
    )(wt_all, x3d, *stacks)


def _log_sigmoid(z):
    return jnp.minimum(z, 0.0) - jnp.log1p(jnp.exp(-jnp.abs(z)))


def _lane_cumsum(x):
    n = x.shape[-1]
    lane = lax.broadcasted_iota(jnp.int32, x.shape, x.ndim - 1)
    shift = 1
    while shift < n:
        x = x + jnp.where(lane >= shift, pltpu.roll(x, shift, x.ndim - 1), 0.0)
        shift *= 2
    return x


def _logf_cum_kernel(cf_ref, bf_ref, logf_ref, cum_ref):
    lf = _log_sigmoid(cf_ref[...] + bf_ref[...])
    lf_t = lf.T[:SUBLANES, :]
    logf_ref[...] = lf_t
    cum_ref[...] = _lane_cumsum(lf_t)


def _logf_cum(proj3, bf_pad):
    b, seq, _ = proj3.shape
    out = jax.ShapeDtypeStruct((b, SUBLANES, seq), F32)
    return pl.pallas_call(
        _logf_cum_kernel,
        out_shape=(out, out),
        grid=(b,),
        in_specs=[pl.BlockSpec((None, seq, LANES), lambda i: (i, 0, BLK['c_f'])),
                  pl.BlockSpec((1, LANES), lambda i: (0, 0))],
        out_specs=(pl.BlockSpec((None, SUBLANES, seq), lambda i: (i, 0, 0)),
                   pl.BlockSpec((None, SUBLANES, seq), lambda i: (i, 0, 0))),
        compiler_params=_cparams(("parallel",)),
        name="logf_cum",
    )(proj3, bf_pad)


def _tri_pairs(n_q, ratio):
    qi = np.array([i for i in range(n_q) for _ in range((i + 1) * ratio)], np.int32)
    kj = np.array([j for i in range(n_q) for j in range((i + 1) * ratio)], np.int32)
    return jnp.asarray(qi), jnp.asarray(kj)


SOFTMAX_ROWS = 32


def _stack_queries(q_ref, q2_ref, tq):
    q = q_ref[...] * (QK_SCALE * LOG2E)
    lo, hi = _lane_half_masks(q.shape)
    q2_ref[0:tq, :] = jnp.where(lo, q, 0.0).astype(BF16)
    q2_ref[tq:, :] = jnp.where(hi, q, 0.0).astype(BF16)


def _flash_init(m_ref, l_ref, acc_ref):
    m_ref[...] = jnp.full(m_ref.shape, NEG_INF, F32)
    l_ref[...] = jnp.zeros(l_ref.shape, F32)
    acc_ref[...] = jnp.zeros(acc_ref.shape, F32)


def _softmax_tile(s_ref, p_ref, m_ref, l_ref, alpha_ref, col_bias, tq, q_minus_k):
    tk = s_ref.shape[1]
    rc = SOFTMAX_ROWS
    reps = tk // LANES
    causal = q_minus_k is not None
    if causal:
        col = lax.broadcasted_iota(jnp.int32, (rc, tk), 1)
        row0 = lax.broadcasted_iota(jnp.int32, (rc, tk), 0) + q_minus_k

    def scores(c):
        s = s_ref[c * rc:(c + 1) * rc, :] + col_bias(c * rc // tq)
        if causal:
            s = jnp.where(col <= row0 + (c * rc) % tq, s, NEG_INF)
        return s

    for c in range(2 * tq // rc):
        rows = slice(c * rc, (c + 1) * rc)
        m_old = m_ref[rows, :]
        m_new = jnp.maximum(m_old, jnp.max(scores(c), axis=1, keepdims=True))
        alpha_ref[rows, :] = jnp.exp2(m_old - m_new)
        m_ref[rows, :] = m_new
    for c in range(2 * tq // rc):
        rows = slice(c * rc, (c + 1) * rc)
        p = jnp.exp2(scores(c) - jnp.concatenate([m_ref[rows, :]] * reps, axis=1))
        l_ref[rows, :] = alpha_ref[rows, :] * l_ref[rows, :] + jnp.sum(p, axis=1, keepdims=True)
        p_ref[rows, :] = p.astype(BF16)


def _flash_steps(i, j, tq, tk, init, step, finish):
    ratio = tq // tk

    @pl.when(j == 0)
    def _():
        init()

    @pl.when(j < i * ratio)
    def _():
        step(None)

    @pl.when(j >= i * ratio)
    def _():
        step(i * tq - j * tk)

    @pl.when(j == (i + 1) * ratio - 1)
    def _():
        finish()


def _lambda(lamv_ref, lam_init):
    lv = lamv_ref[...]
    e1 = jnp.exp(jnp.sum(lv[0:1] * lv[1:2], axis=1, keepdims=True))
    e2 = jnp.exp(jnp.sum(lv[2:3] * lv[3:4], axis=1, keepdims=True))
    return e1 - e2 + lam_init


def _attn_a_kernel(qi_ref, kj_ref, slopes_ref, lamv_ref, g_ref, q_ref, k_ref, v_ref, gate_ref,
                   o_ref, q2_ref, s_ref, p_ref, m_ref, l_ref, alpha_ref, acc_ref, *, tq, tk, lam_init):
    h = pl.program_id(1)
    p = pl.program_id(2)
    i = qi_ref[p]
    j = kj_ref[p]

    def init():
        _stack_queries(q_ref, q2_ref, tq)
        _flash_init(m_ref, l_ref, acc_ref)

    def step(q_minus_k):
        s_ref[...] = _dot_nt(q2_ref[...], k_ref[...])
        kpos = lax.broadcasted_iota(jnp.int32, (1, tk), 1) + j * tk
        bias = (slopes_ref[h] * LOG2E) * kpos.astype(F32)
        _softmax_tile(s_ref, p_ref, m_ref, l_ref, alpha_ref, lambda half: bias, tq, q_minus_k)
        acc_ref[...] = alpha_ref[...] * acc_ref[...] + _dot(p_ref[...], v_ref[...])

    def finish():
        o = acc_ref[...] / l_ref[...]
        o = o[:tq] - _lambda(lamv_ref, lam_init) * o[tq:]
        y = o * lax.rsqrt(jnp.mean(o * o, axis=1, keepdims=True) + LN_EPS) * g_ref[...]
        o_ref[...] = y * (1.0 - lam_init) * _silu(gate_ref[...])

    _flash_steps(i, j, tq, tk, init, step, finish)


def _attn_a(proj3, slopes, lamv, g, lam_init, tq, tk):
    b, seq, _ = proj3.shape
    qi, kj = _tri_pairs(seq // tq, tq // tk)

    def blk(col, rows, tile):
        return pl.BlockSpec((None, rows, LANES),
                            lambda bb, h, p, qi, kj: (bb, (qi if tile == 'q' else kj)[p], col + h))

    kern = functools.partial(_attn_a_kernel, tq=tq, tk=tk, lam_init=lam_init)
    return pl.pallas_call(
        kern,
        out_shape=jax.ShapeDtypeStruct((b, seq, W_A), F32),
        grid_spec=pltpu.PrefetchScalarGridSpec(
            num_scalar_prefetch=2,
            grid=(b, H_A, qi.shape[0]),
            in_specs=[pl.BlockSpec(memory_space=pltpu.SMEM),
                      pl.BlockSpec((4, HEAD_DIM), lambda bb, h, p, qi, kj: (0, 0)),
                      pl.BlockSpec((1, LANES), lambda bb, h, p, qi, kj: (0, 0)),
                      blk(BLK['a_q'], tq, 'q'), blk(BLK['a_k'], tk, 'k'), blk(BLK['a_v'], tk, 'k'),
                      blk(BLK['a_g'], tq, 'q')],
            out_specs=pl.BlockSpec((None, tq, LANES), lambda bb, h, p, qi, kj: (bb, qi[p], h)),
            scratch_shapes=_flash_scratch(tq, tk)),
        compiler_params=_cparams(("parallel", "parallel", "arbitrary")),
        name="attn_a",
    )(qi, kj, slopes, lamv, g, proj3, proj3, proj3, proj3)


def _flash_scratch(tq, tk):
    rep = pltpu.VMEM((2 * tq, LANES), F32)
    return [pltpu.VMEM((2 * tq, LANES), BF16),
            pltpu.VMEM((2 * tq, tk), F32),
            pltpu.VMEM((2 * tq, tk), BF16),
            rep, rep, rep, rep]


def _attn_c_kernel(qi_ref, kj_ref, q_ref, kt_ref, vt_ref, cum_ref, gate_ref,
                   o_ref, q2_ref, s_ref, p_ref, m_ref, l_ref, alpha_ref, acc_ref, *, tq, tk):
    hb = pl.program_id(1)
    p = pl.program_id(2)
    i = qi_ref[p]
    j = kj_ref[p]

    def init():
        _stack_queries(q_ref, q2_ref, tq)
        _flash_init(m_ref, l_ref, acc_ref)

    def step(q_minus_k):
        s_ref[...] = _dot(q2_ref[...], kt_ref[...])
        bias = [cum_ref[pl.ds(2 * hb + half, 1), :] * (-LOG2E) for half in range(2)]
        _softmax_tile(s_ref, p_ref, m_ref, l_ref, alpha_ref, lambda half: bias[half], tq, q_minus_k)
        acc_ref[...] = alpha_ref[...] * acc_ref[...] + _dot_nt(p_ref[...], vt_ref[...])

    def finish():
        o = acc_ref[...] / l_ref[...]
        lo, _ = _lane_half_masks((tq, LANES))
        o_ref[...] = jnp.where(lo, o[:tq], o[tq:]) * _silu(gate_ref[...])

    _flash_steps(i, j, tq, tk, init, step, finish)


def _attn_c(proj3, kt_all, vt_all, layer, cum_t, tq, tk):
    b, seq, _ = proj3.shape
    qi, kj = _tri_pairs(seq // tq, tq // tk)
    n_pairs = H_C // 2

    def blk(col):
        return pl.BlockSpec((None, tq, LANES), lambda bb, h, p, qi, kj: (bb, qi[p], col + h))

    blk_t = pl.BlockSpec((None, None, LANES, tk), lambda bb, h, p, qi, kj: (layer, bb, h, kj[p]))

    return pl.pallas_call(
        functools.partial(_attn_c_kernel, tq=tq, tk=tk),
        out_shape=jax.ShapeDtypeStruct((b, seq, W_C), F32),
        grid_spec=pltpu.PrefetchScalarGridSpec(
            num_scalar_prefetch=2,
            grid=(b, n_pairs, qi.shape[0]),
            in_specs=[blk(BLK['c_q']), blk_t, blk_t,
                      pl.BlockSpec((None, SUBLANES, tk), lambda bb, h, p, qi, kj: (bb, 0, kj[p])),
                      blk(BLK['c_g'])],
            out_specs=pl.BlockSpec((None, tq, LANES), lambda bb, h, p, qi, kj: (bb, qi[p], h)),
            scratch_shapes=_flash_scratch(tq, tk)),
        compiler_params=_cparams(("parallel", "parallel", "arbitrary")),
        name="attn_c",
    )(qi, kj, proj3, kt_all, vt_all, cum_t, proj3)


def _ret_tables(chunk, n_valid):
    log_gamma = jnp.log(1.0 - 2.0 ** (-5.0 - jnp.arange(H_B, dtype=F32)))
    idx = jnp.arange(chunk, dtype=F32)
    valid = idx < n_valid
    rel = idx[:, None] - idx[None, :]
    decay = jnp.where(rel >= 0, jnp.exp(log_gamma[:, None, None] * jnp.maximum(rel, 0.0)), 0.0)
    key_rows = max(chunk, LANES)
    decay = jnp.pad(decay, ((0, 0), (0, 0), (0, key_rows - chunk)))
    k_w = jnp.where(valid, jnp.exp(log_gamma[:, None] * (n_valid - 1 - idx)[None, :]), 0.0)
    q_w = jnp.exp(log_gamma[:, None] * (idx + 1)[None, :])
    g = jnp.exp(log_gamma * n_valid)

    def lanes(t):
        t = t.reshape(H_B // 2, 2, chunk)
        return jnp.repeat(jnp.transpose(t, (0, 2, 1)), HEAD_DIM, axis=2)

    g_rows = jnp.repeat(g.reshape(H_B // 2, 2), HEAD_DIM, axis=1)
    g_tab = jnp.broadcast_to(g_rows[:, :, None], (H_B // 2, LANES, LANES))
    return decay.reshape(H_B // 2, 2, chunk, key_rows), lanes(k_w) * QK_SCALE, lanes(q_w), g_tab


def _retention_kernel(q_ref, k_ref, v_ref, gate_ref, s0_ref, decay_ref, kw_ref, qw_ref, g_ref,
                      o_ref, s_out_ref, s_ref, *, chunk, n_chunks):
    t = pl.program_id(2)

    @pl.when(t == 0)
    def _():
        s_ref[...] = s0_ref[...]

    shape = (chunk, LANES)
    lo, hi = _lane_half_masks(shape)
    r = lax.broadcasted_iota(jnp.int32, (LANES, LANES), 0)
    c = lax.broadcasted_iota(jnp.int32, (LANES, LANES), 1)
    same_head = (r < HEAD_DIM) == (c < HEAD_DIM)

    for n in range(n_chunks):
        rows = slice(n * chunk, (n + 1) * chunk)
        q = q_ref[rows, :]
        k = k_ref[rows, :]
        v = v_ref[rows, :]
        kw = k * kw_ref[...]
        if chunk < LANES:
            pad = jnp.zeros((LANES - chunk, LANES), F32)
            k, v, kw = (jnp.concatenate([a, pad], axis=0) for a in (k, v, kw))
        lo_k, hi_k = _lane_half_masks(v.shape)
        state = s_ref[...]
        o_cross = _dot(q * qw_ref[...], state)
        a_lo = _dot_nt(jnp.where(lo, q, 0.0), k) * (QK_SCALE * decay_ref[0])
        a_hi = _dot_nt(jnp.where(hi, q, 0.0), k) * (QK_SCALE * decay_ref[1])
        o = o_cross + _dot(a_lo, jnp.where(lo_k, v, 0.0)) + _dot(a_hi, jnp.where(hi_k, v, 0.0))
        kv = _dot(kw.T, v)
        s_ref[...] = g_ref[...] * state + jnp.where(same_head, kv, 0.0)
        sq = o * o
        ms_lo = jnp.sum(jnp.where(lo, sq, 0.0), axis=1, keepdims=True) * (1.0 / HEAD_DIM)
        ms_hi = jnp.sum(jnp.where(hi, sq, 0.0), axis=1, keepdims=True) * (1.0 / HEAD_DIM)
        y = o * lax.rsqrt(jnp.where(lo, ms_lo, ms_hi) + LN_EPS)
        o_ref[rows, :] = y * _silu(gate_ref[rows, :])

    @pl.when(t == pl.num_programs(2) - 1)
    def _():
        s_out_ref[...] = s_ref[...]


def _retention(proj3, s0_bd, tables, chunk, rows_per_step):
    b, seq, _ = proj3.shape
    decay, kw, qw, g_tab = tables
    n_pairs = H_B // 2
    n_chunks = rows_per_step // chunk

    def blk(col):
        return pl.BlockSpec((None, rows_per_step, LANES), lambda bb, h, t: (bb, t, col + h))

    def tab(shape):
        return pl.BlockSpec((None,) + shape, lambda bb, h, t: (h,) + (0,) * len(shape))

    return pl.pallas_call(
        functools.partial(_retention_kernel, chunk=chunk, n_chunks=n_chunks),
        out_shape=(jax.ShapeDtypeStruct((b, seq, W_B), F32),
                   jax.ShapeDtypeStruct((b, n_pairs, LANES, LANES), F32)),
        grid=(b, n_pairs, seq // rows_per_step),
        in_specs=[blk(BLK['b_q']), blk(BLK['b_k']), blk(BLK['b_v']), blk(BLK['b_g']),
                  pl.BlockSpec((None, None, LANES, LANES), lambda bb, h, t: (bb, h, 0, 0)),
                  tab((2, chunk, max(chunk, LANES))), tab((chunk, LANES)), tab((chunk, LANES)),
                  tab((LANES, LANES))],
        out_specs=(pl.BlockSpec((None, rows_per_step, LANES), lambda bb, h, t: (bb, t, h)),
                   pl.BlockSpec((None, None, LANES, LANES), lambda bb, h, t: (bb, h, 0, 0))),
        scratch_shapes=[pltpu.VMEM((LANES, LANES), F32)],
        compiler_params=_cparams(("parallel", "parallel", "arbitrary")),
        name="retention",
    )(proj3, proj3, proj3, proj3, s0_bd, decay, kw, qw, g_tab)


def _state_to_blockdiag(s):
    b = s.shape[0]
    s = s.reshape(b, H_B // 2, 2, HEAD_DIM, HEAD_DIM)
    z = jnp.zeros_like(s[:, :, 0])
    top = jnp.concatenate([s[:, :, 0], z], axis=-1)
    bot = jnp.concatenate([z, s[:, :, 1]], axis=-1)
    return jnp.concatenate([top, bot], axis=-2)


def _state_from_blockdiag(s):
    b = s.shape[0]
    even = s[:, :, :HEAD_DIM, :HEAD_DIM]
    odd = s[:, :, HEAD_DIM:, HEAD_DIM:]
    return jnp.stack([even, odd], axis=2).reshape(b, H_B, HEAD_DIM, HEAD_DIM)


def _mem_attn_kernel(q_ref, gate_ref, mk_ref, mv_ref, o_ref):
    q = q_ref[...] * QK_SCALE
    lo, hi = _lane_half_masks((q.shape[0], LANES))
    outs = []
    for pair in range(H_M // 2):
        cols = slice(pair * LANES, (pair + 1) * LANES)
        qp = q[:, cols]
        mk = mk_ref[:, cols]
        mv = mv_ref[:, cols]
        o_pair = None
        for mask in (lo, hi):
            s = _dot_nt(jnp.where(mask, qp, 0.0), mk)
            p = jnp.exp(s - jnp.max(s, axis=1, keepdims=True))
            o = _dot(p, mv) / jnp.sum(p, axis=1, keepdims=True)
            o_pair = jnp.where(mask, o, 0.0) if o_pair is None else jnp.where(mask, o, o_pair)
        outs.append(o_pair)
    o_ref[...] = jnp.concatenate(outs, axis=1) * _silu(gate_ref[...])


def _mem_attn_t_kernel(q_ref, gate_ref, mkt_ref, mvt_ref, o_ref):
    q = q_ref[...] * QK_SCALE
    col = lax.broadcasted_iota(jnp.int32, q.shape, 1)
    o_all = None
    for h in range(H_M):
        head = col // HEAD_DIM == h
        s = _dot(jnp.where(head, q, 0.0), mkt_ref[...])
        p = jnp.exp(s - jnp.max(s, axis=1, keepdims=True))
        o = _dot_nt(p, mvt_ref[...]) / jnp.sum(p, axis=1, keepdims=True)
        o_all = jnp.where(head, o, 0.0) if o_all is None else jnp.where(head, o, o_all)
    o_ref[...] = o_all * _silu(gate_ref[...])


def _mem_attn(proj3, mk_arr, mv_arr, mk_blk, mv_blk, tq, transposed, mem_off=0):
    b, seq, _ = proj3.shape
    mem_block = (None, W_M, mk_arr.shape[2]) if transposed else (None, mk_arr.shape[1], W_M)
    return pl.pallas_call(
        _mem_attn_t_kernel if transposed else _mem_attn_kernel,
        out_shape=jax.ShapeDtypeStruct((b, seq, W_M), F32),
        grid=(b, seq // tq),
        in_specs=[pl.BlockSpec((None, tq, W_M), lambda bb, i: (bb, i, COL['m_q'] // W_M)),
                  pl.BlockSpec((None, tq, W_M), lambda bb, i: (bb, i, COL['m_g'] // W_M)),
                  pl.BlockSpec(mem_block, lambda bb, i: (bb + mem_off, 0, mk_blk)),
                  pl.BlockSpec(mem_block, lambda bb, i: (bb + mem_off, 0, mv_blk))],
        out_specs=pl.BlockSpec((None, tq, W_M), lambda bb, i: (bb, i, 0)),
        compiler_params=_cparams(("parallel", "parallel")),
        name="mem_attn",
    )(proj3, proj3, mk_arr, mv_arr)


def _out_ln_kernel(oa_ref, ob_ref, oc_ref, om_ref, w_ref, x_ref, g_ref, b_ref, y_ref, *, alpha):
    y = _dot(oa_ref[...], w_ref[0:W_A, :])
    y += _dot(ob_ref[...], w_ref[W_A:W_A + W_B, :])
    y += _dot(oc_ref[...], w_ref[W_A + W_B:W_A + W_B + W_C, :])
    y += _dot(om_ref[...], w_ref[W_A + W_B + W_C:, :])
    z = alpha * x_ref[...] + y
    mu = jnp.mean(z, axis=1, keepdims=True)
    zc = z - mu
    var = jnp.mean(zc * zc, axis=1, keepdims=True)
    y_ref[...] = zc * lax.rsqrt(var + LN_EPS) * g_ref[...] + b_ref[...]


def _out_ln(o_a, o_b, o_c, o_m, w_out_all, layer, x2d, ln_g, ln_b, alpha, tm):
    m, d = x2d.shape

    def rows(w):
        return pl.BlockSpec((tm, w), lambda i: (i, 0))

    return pl.pallas_call(
        functools.partial(_out_ln_kernel, alpha=alpha),
        out_shape=jax.ShapeDtypeStruct((m, d), F32),
        grid=(m // tm,),
        in_specs=[rows(W_A), rows(W_B), rows(W_C), rows(W_M),
                  pl.BlockSpec((None, MIX, d), lambda i: (layer, 0, 0)),
                  rows(d),
                  pl.BlockSpec((None, 1, d), lambda i: (layer, 0, 0)),
                  pl.BlockSpec((None, 1, d), lambda i: (layer, 0, 0))],
        out_specs=rows(d),
        compiler_params=_cparams(("parallel",)),
        name="out_ln",
    )(o_a, o_b, o_c, o_m, w_out_all, x2d, ln_g, ln_b)


PAGES_PER_STEP = 32
ROWS_S = 8


def _split3(x):
    hi = x.astype(BF16).astype(F32)
    mid = (x - hi).astype(BF16).astype(F32)
    lo = x - hi - mid
    return hi, mid, lo


def _online_merge(s, m_ref, l_ref):
    m_old = m_ref[...]
    m_new = jnp.maximum(m_old, jnp.max(s, axis=1, keepdims=True))
    alpha = jnp.exp(m_old - m_new)
    p = jnp.exp(s - m_new)
    l_ref[...] = alpha * l_ref[...] + jnp.sum(p, axis=1, keepdims=True)
    m_ref[...] = m_new
    return alpha, p


def _rows_from_tokens(x, n_new):
    return jnp.concatenate([jnp.broadcast_to(x[t:t + 1], (SUBLANES, x.shape[1])) for t in range(n_new)],
                           axis=0)


def _slot_sum(x, n_new):
    return jnp.sum(x.reshape(n_new, SUBLANES, x.shape[1]), axis=1)


def _dec_a_kernel(pt_ref, slopes_ref, lamv_ref, g_ref, q_ref, kn_ref, vn_ref, gate_ref, *rest,
                  n_new, n_past, page, lam_init):
    k_pages = rest[:PAGES_PER_STEP]
    v_pages = rest[PAGES_PER_STEP:2 * PAGES_PER_STEP]
    o_ref, q_rows_ref, m_ref, l_ref, acc_ref = rest[2 * PAGES_PER_STEP:]
    g = pl.program_id(1)
    n_rows = n_new * SUBLANES
    row = lax.broadcasted_iota(jnp.int32, (n_rows, 1), 0)
    t_row = row // SUBLANES
    h_row = (row % SUBLANES) // 2
    slope_row = jnp.zeros((n_rows, 1), F32)
    for h in range(H_A):
        slope_row = jnp.where(h_row == h, slopes_ref[h], slope_row)

    def head_rows(x):
        out = jnp.zeros((n_rows, LANES), F32)
        for h in range(H_A):
            out = jnp.where(h_row == h, _rows_from_tokens(x[:, h * LANES:(h + 1) * LANES], n_new), out)
        return out

    @pl.when(g == 0)
    def _():
        q = head_rows(q_ref[...] * QK_SCALE)
        lane = lax.broadcasted_iota(jnp.int32, q.shape, 1)
        q_rows_ref[...] = jnp.where(lane // HEAD_DIM == row % 2, q, 0.0)
        _flash_init(m_ref, l_ref, acc_ref)

    q_rows = q_rows_ref[...]
    col = lax.broadcasted_iota(jnp.int32, (n_rows, page * H_A), 1)
    own = (col % H_A) == h_row
    rel = (col // H_A - (n_past + t_row)).astype(F32)
    bias0 = jnp.where(own, slope_row * rel, NEG_INF)
    s_parts = []
    for n, kp in enumerate(k_pages):
        base = ((g * PAGES_PER_STEP + n) * page).astype(F32)
        s_parts.append(_dot_nt(q_rows, kp[...]) + (bias0 + slope_row * base))
    alpha, p = _online_merge(jnp.concatenate(s_parts, axis=1), m_ref, l_ref)
    acc = alpha * acc_ref[...]
    w = page * H_A
    for n, vp in enumerate(v_pages):
        acc += _dot(p[:, n * w:(n + 1) * w], vp[...])
    acc_ref[...] = acc

    @pl.when(g == pl.num_programs(1) - 1)
    def _():
        kn = kn_ref[...]
        vn = vn_ref[...]
        s_new = []
        for u in range(n_new):
            ku = head_rows(jnp.broadcast_to(kn[u:u + 1], (n_new, W_A)))
            su = jnp.sum(q_rows * ku, axis=1, keepdims=True) + slope_row * (u - t_row).astype(F32)
            s_new.append(jnp.where(t_row >= u, su, NEG_INF))
        alpha, p = _online_merge(jnp.concatenate(s_new, axis=1), m_ref, l_ref)
        acc = alpha * acc_ref[...]
        for u in range(n_new):
            acc += p[:, u:u + 1] * head_rows(jnp.broadcast_to(vn[u:u + 1], (n_new, W_A)))
        o = acc / l_ref[...]
        o = o * jnp.where(row % 2 == 0, 1.0, -_lambda(lamv_ref, lam_init))
        outs = []
        for h in range(H_A):
            oh = _slot_sum(jnp.where(h_row == h, o, 0.0), n_new)
            yh = oh * lax.rsqrt(jnp.mean(oh * oh, axis=1, keepdims=True) + LN_EPS) * g_ref[...]
            outs.append(yh * (1.0 - lam_init))
        y = jnp.concatenate(outs, axis=1) * _silu(gate_ref[...][:n_new])
        o_ref[...] = jnp.concatenate([y, jnp.zeros((ROWS_S - n_new, W_A), F32)], axis=0)


def _dec_a(proj3, cache_k, cache_v, layer, pt_flat, slopes, lamv, g, lam_init, n_new):
    b = proj3.shape[0]
    page = cache_k.shape[2] // H_A
    n_pages = pt_flat.shape[0] // b
    n_steps = n_pages // PAGES_PER_STEP

    def new_blk(col):
        return pl.BlockSpec((None, ROWS_S, W_A), lambda bb, s, pt: (bb, 0, col // W_A))

    def page_blk(n):
        return pl.BlockSpec((None, None, page * H_A, LANES),
                            lambda bb, s, pt: (layer, pt[bb * n_pages + s * PAGES_PER_STEP + n], 0, 0))

    n_rows = n_new * SUBLANES
    kern = functools.partial(_dec_a_kernel, n_new=n_new, n_past=n_pages * page, page=page,
                             lam_init=lam_init)
    return pl.pallas_call(
        kern,
        out_shape=jax.ShapeDtypeStruct((b, ROWS_S, W_A), F32),
        grid_spec=pltpu.PrefetchScalarGridSpec(
            num_scalar_prefetch=1,
            grid=(b, n_steps),
            in_specs=[pl.BlockSpec(memory_space=pltpu.SMEM),
                      pl.BlockSpec((4, HEAD_DIM), lambda bb, s, pt: (0, 0)),
                      pl.BlockSpec((1, LANES), lambda bb, s, pt: (0, 0)),
                      new_blk(COL['a_q']), new_blk(COL['a_k']), new_blk(COL['a_v']),
                      new_blk(COL['a_g'])]
                     + [page_blk(n) for n in range(PAGES_PER_STEP)] * 2,
            out_specs=pl.BlockSpec((None, ROWS_S, W_A), lambda bb, s, pt: (bb, 0, 0)),
            scratch_shapes=[pltpu.VMEM((n_rows, LANES), F32),
                            pltpu.VMEM((n_rows, 1), F32),
                            pltpu.VMEM((n_rows, 1), F32),
                            pltpu.VMEM((n_rows, LANES), F32)]),
        compiler_params=_cparams(("parallel", "arbitrary")),
        name="dec_a",
    )(pt_flat, slopes, lamv, g, proj3, proj3, proj3, proj3,
      *([cache_k] * PAGES_PER_STEP), *([cache_v] * PAGES_PER_STEP))


def _dec_c_kernel(pt_ref, bf_ref, q_ref, kn_ref, vn_ref, cf_ref, gate_ref, *rest,
                  n_new, page, n_pages):
    k_pages = rest[:PAGES_PER_STEP]
    v_pages = rest[PAGES_PER_STEP:2 * PAGES_PER_STEP]
    f_pages = rest[2 * PAGES_PER_STEP:3 * PAGES_PER_STEP]
    o_ref, logf_ref, qbd_ref, m_ref, l_ref, acc_ref, run_ref = rest[3 * PAGES_PER_STEP:]
    b = pl.program_id(0)
    g = pl.program_id(1)
    n_rows = n_new * SUBLANES
    row = lax.broadcasted_iota(jnp.int32, (n_rows, 1), 0)
    t_row = row // SUBLANES
    h_row = row % SUBLANES

    @pl.when(g == 0)
    def _():
        qr = _rows_from_tokens(q_ref[...] * QK_SCALE, n_new)
        col = lax.broadcasted_iota(jnp.int32, (n_rows, W_C), 1)
        qbd_ref[...] = jnp.where(col // HEAD_DIM == h_row, qr, 0.0)
        _flash_init(m_ref, l_ref, acc_ref)
        run_ref[...] = jnp.zeros(run_ref.shape, F32)

    qbd = qbd_ref[...]
    n_r = PAGES_PER_STEP * SUBLANES
    lf = jnp.concatenate([fp[...] for fp in f_pages], axis=0)
    tok_r = lax.broadcasted_iota(jnp.int32, (page, page), 0)
    tok_c = lax.broadcasted_iota(jnp.int32, (page, page), 1)
    y = _dot(jnp.concatenate(_split3(lf), axis=0), jnp.where(tok_r <= tok_c, 1.0, 0.0))
    cum = y[:n_r] + y[n_r:2 * n_r] + y[2 * n_r:]
    r = lax.broadcasted_iota(jnp.int32, (n_r, n_r), 0)
    c = lax.broadcasted_iota(jnp.int32, (n_r, n_r), 1)
    earlier = jnp.where((r % SUBLANES == c % SUBLANES) & (c < r), 1.0, 0.0)
    tot = jnp.broadcast_to(cum[:, page - 1:page], (n_r, LANES))
    y = _dot(earlier, jnp.concatenate(_split3(tot), axis=1))
    off = y[:, :LANES] + y[:, LANES:2 * LANES] + y[:, 2 * LANES:]
    cum = cum + off + jnp.concatenate([run_ref[...]] * PAGES_PER_STEP, axis=0)
    run = cum[n_r - SUBLANES:, page - 1:page]
    run_ref[...] = run
    s_parts = []
    for n, kp in enumerate(k_pages):
        cum_n = cum[n * SUBLANES:(n + 1) * SUBLANES]
        s_parts.append(_dot(qbd, kp[...]) - jnp.concatenate([cum_n] * n_new, axis=0))
    alpha, p = _online_merge(jnp.concatenate(s_parts, axis=1), m_ref, l_ref)
    acc = alpha * acc_ref[...]
    for n, vp in enumerate(v_pages):
        acc += _dot_nt(p[:, n * page:(n + 1) * page], vp[...])
    acc_ref[...] = acc

    @pl.when(g == pl.num_programs(1) - 1)
    def _():
        kn = kn_ref[...]
        vn = vn_ref[...]
        lf = _log_sigmoid(cf_ref[...] + bf_ref[...])
        logf_ref[...] = lf
        lane = lax.broadcasted_iota(jnp.int32, (n_rows, LANES), 1)
        pick = lane == h_row
        run_rows = jnp.concatenate([run] * n_new, axis=0)
        cum_new = jnp.zeros((1, LANES), F32)
        s_new = []
        for u in range(n_new):
            cum_new = cum_new + lf[u:u + 1]
            cu = jnp.sum(jnp.where(pick, cum_new, 0.0), axis=1, keepdims=True)
            su = jnp.sum(qbd * kn[u:u + 1], axis=1, keepdims=True) - (run_rows + cu)
            s_new.append(jnp.where(t_row >= u, su, NEG_INF))
        alpha, p = _online_merge(jnp.concatenate(s_new, axis=1), m_ref, l_ref)
        acc = alpha * acc_ref[...]
        for u in range(n_new):
            acc += p[:, u:u + 1] * vn[u:u + 1]
        o = acc / l_ref[...]
        col = lax.broadcasted_iota(jnp.int32, o.shape, 1)
        o = _slot_sum(jnp.where(col // HEAD_DIM == h_row, o, 0.0), n_new)
        y = o * _silu(gate_ref[...][:n_new])
        o_ref[...] = jnp.concatenate([y, jnp.zeros((ROWS_S - n_new, W_C), F32)], axis=0)


def _dec_c(proj3, cache_kt, cache_vt, cache_f, layer, pt_flat, bf_pad, n_new):
    b = proj3.shape[0]
    page = cache_kt.shape[3]
    n_pages = pt_flat.shape[0] // b
    n_steps = n_pages // PAGES_PER_STEP

    def new_blk(col, w):
        return pl.BlockSpec((None, ROWS_S, w), lambda bb, s, pt: (bb, 0, col // w))

    def page_idx(bb, s, pt, n):
        return pt[bb * n_pages + s * PAGES_PER_STEP + n]

    def kv_blk(n):
        return pl.BlockSpec((None, None, W_C, page),
                            lambda bb, s, pt: (layer, page_idx(bb, s, pt, n), 0, 0))

    def f_blk(n):
        return pl.BlockSpec((None, None, SUBLANES, page),
                            lambda bb, s, pt: (layer, page_idx(bb, s, pt, n), 0, 0))

    n_rows = n_new * SUBLANES
    return pl.pallas_call(
        functools.partial(_dec_c_kernel, n_new=n_new, page=page, n_pages=n_pages),
        out_shape=(jax.ShapeDtypeStruct((b, ROWS_S, W_C), F32),
                   jax.ShapeDtypeStruct((b, ROWS_S, LANES), F32)),
        grid_spec=pltpu.PrefetchScalarGridSpec(
            num_scalar_prefetch=1,
            grid=(b, n_steps),
            in_specs=[pl.BlockSpec((1, LANES), lambda bb, s, pt: (0, 0)),
                      new_blk(COL['c_q'], W_C), new_blk(COL['c_k'], W_C), new_blk(COL['c_v'], W_C),
                      new_blk(COL['c_f'], LANES), new_blk(COL['c_g'], W_C)]
                     + [kv_blk(n) for n in range(PAGES_PER_STEP)] * 2
                     + [f_blk(n) for n in range(PAGES_PER_STEP)],
            out_specs=(pl.BlockSpec((None, ROWS_S, W_C), lambda bb, s, pt: (bb, 0, 0)),
                       pl.BlockSpec((None, ROWS_S, LANES), lambda bb, s, pt: (bb, 0, 0))),
            scratch_shapes=[pltpu.VMEM((n_rows, W_C), F32),
                            pltpu.VMEM((n_rows, 1), F32),
                            pltpu.VMEM((n_rows, 1), F32),
                            pltpu.VMEM((n_rows, W_C), F32),
                            pltpu.VMEM((SUBLANES, 1), F32)]),
        compiler_params=_cparams(("parallel", "arbitrary")),
        name="dec_c",
    )(pt_flat, bf_pad, proj3, proj3, proj3, proj3, proj3,
      *([cache_kt] * PAGES_PER_STEP), *([cache_vt] * PAGES_PER_STEP), *([cache_f] * PAGES_PER_STEP))


def _pick_tile(n, prefs):
    for t in prefs:
        if n % t == 0:
            return t
    return n


def kernel(x_prompt, x_sample, cache_a_k, cache_a_v, cache_c_k, cache_c_v, cache_c_logf, cache_mem_k, cache_mem_v, state_ret, page_table, mem_prompt, w_in, b_f, lam_q1, lam_k1, lam_q2, lam_k2, a_subln_g, w_out, w_mem_kv, ln_g, ln_b):
    depth, d_model, _ = w_in.shape
    bp, seq, _ = x_prompt.shape
    bs, n_new, _ = x_sample.shape
    n_phys, page = cache_a_k.shape[1], cache_a_k.shape[2]
    n_mem = mem_prompt.shape[1]
    alpha = (2 * depth) ** 0.25
    assert n_new <= ROWS_S and page_table.shape[1] % PAGES_PER_STEP == 0
    assert page == LANES

    w_perm = _permute_w_in(w_in).astype(BF16)
    ck0 = REF_COL['c_k'][0]
    assert REF_COL['c_v'][0] == ck0 + W_C
    wt_ckv = jnp.swapaxes(w_in[:, :, ck0:ck0 + 2 * W_C], 1, 2).astype(BF16)
    w_out_b = w_out.astype(BF16)
    w_mem_b = w_mem_kv.astype(BF16)
    bf_pad = jnp.pad(b_f, ((0, 0), (0, LANES - H_C)))[:, None, :]
    lamv = jnp.stack([lam_q1, lam_k1, lam_q2, lam_k2], axis=1)
    g_a = a_subln_g[:, None, :]
    ln_g3, ln_b3 = ln_g[:, None, :], ln_b[:, None, :]
    slopes = 2.0 ** (-8.0 * jnp.arange(1, H_A + 1, dtype=F32) / H_A)
    ck_a = cache_a_k.reshape(depth, n_phys, page * H_A, 2 * HEAD_DIM)
    cv_a = cache_a_v.reshape(depth, n_phys, page * H_A, 2 * HEAD_DIM)
    ck_c = jnp.transpose(cache_c_k, (0, 1, 3, 4, 2)).reshape(depth, n_phys, W_C, page)
    cv_c = jnp.transpose(cache_c_v, (0, 1, 3, 4, 2)).reshape(depth, n_phys, W_C, page)
    cf_c = jnp.pad(jnp.transpose(cache_c_logf, (0, 1, 3, 2)),
                   ((0, 0), (0, 0), (0, SUBLANES - H_C), (0, 0)))
    pt_flat = page_table.reshape(-1)
    mem2d = mem_prompt.reshape(bp * n_mem, d_model)
    smk = jnp.transpose(cache_mem_k, (0, 1, 3, 4, 2)).reshape(depth * bs, W_M, n_mem)
    smv = jnp.transpose(cache_mem_v, (0, 1, 3, 4, 2)).reshape(depth * bs, W_M, n_mem)

    chunk_p = RET_CHUNK if seq % RET_CHUNK == 0 else seq
    tables_p = _ret_tables(chunk_p, chunk_p)
    tables_s = _ret_tables(ROWS_S, n_new)
    s0_p = jnp.zeros((bp, H_B // 2, LANES, LANES), F32)

    tq = _pick_tile(seq, (1024, 512, 256, 128))
    tk = _pick_tile(tq, (512, 256, 128))
    tm_p = _pick_tile(bp * seq, (1024, 512, 256, 128))
    tn = _pick_tile(PW, (1152, 640, 128))
    tn_main = _pick_tile(PW_MAIN, (1664, 384, 128))
    tm_t = _pick_tile(seq, (1024, 512, 256, 128))
    ret_rows = _pick_tile(seq, (1024, 512, 256, 128))
    tq_m = _pick_tile(seq, (1024, 512, 256, 128))

    xp = x_prompt.reshape(bp * seq, d_model)
    xs = jnp.pad(x_sample, ((0, 0), (0, ROWS_S - n_new), (0, 0))).reshape(bs * ROWS_S, d_model)

    outs_p = [[] for _ in range(4)]
    a_stacks, c_stacks = [], []
    outs_s = [[] for _ in range(6)]
    for l in range(depth):
        lam_init = 0.8 - 0.6 * math.exp(-0.3 * l)

        pj, *a_stacks = _proj_main(xp, w_perm, l, tm_p, tn_main, a_stacks)
        pj = pj.reshape(bp, seq, PW_MAIN)
        c_stacks = _proj_t(xp.reshape(bp, seq, d_model), wt_ckv, l, tm_t, c_stacks)
        logf_t, cum_t = _logf_cum(pj, bf_pad[l])
        o_a = _attn_a(pj, slopes, lamv[l], g_a[l], lam_init, tq, tk)
        o_c = _attn_c(pj, c_stacks[0], c_stacks[1], l, cum_t, tq, tk)
        o_b, s_new = _retention(pj, s0_p, tables_p, chunk_p, ret_rows)
        mkv = _proj(mem2d, w_mem_b, l, _pick_tile(bp * n_mem, (512, 256, 128)), W_M).reshape(bp, n_mem, 2 * W_M)
        o_m = _mem_attn(pj, mkv, mkv, 0, 1, tq_m, False)
        xp = _out_ln(o_a.reshape(bp * seq, W_A), o_b.reshape(bp * seq, W_B), o_c.reshape(bp * seq, W_C),
                     o_m.reshape(bp * seq, W_M), w_out_b, l, xp, ln_g3, ln_b3, alpha,
                     _pick_tile(bp * seq, (512, 256, 128)))
        outs_p[0].append(jnp.transpose(logf_t[:, :H_C, :], (0, 2, 1)))
        outs_p[1].append(_state_from_blockdiag(s_new))
        outs_p[2].append(mkv[:, :, :W_M].reshape(bp, n_mem, H_M, HEAD_DIM))
        outs_p[3].append(mkv[:, :, W_M:].reshape(bp, n_mem, H_M, HEAD_DIM))

        pjs = _proj(xs, w_perm, l, _pick_tile(bs * ROWS_S, (256, 128, 8)), tn).reshape(bs, ROWS_S, PW)
        o_a = _dec_a(pjs, ck_a, cv_a, l, pt_flat, slopes, lamv[l], g_a[l], lam_init, n_new)
        o_c, logf_s = _dec_c(pjs, ck_c, cv_c, cf_c, l, pt_flat, bf_pad[l], n_new)
        o_b, s_new = _retention(pjs, _state_to_blockdiag(state_ret[l]), tables_s, ROWS_S, ROWS_S)
        o_m = _mem_attn(pjs, smk, smv, 0, 0, ROWS_S, True, mem_off=l * bs)
        xs = _out_ln(o_a.reshape(bs * ROWS_S, W_A), o_b.reshape(bs * ROWS_S, W_B),
                     o_c.reshape(bs * ROWS_S, W_C), o_m.reshape(bs * ROWS_S, W_M), w_out_b, l, xs,
                     ln_g3, ln_b3, alpha, _pick_tile(bs * ROWS_S, (256, 128, 8)))
        outs_s[0].append(pjs[:, :n_new, COL['a_k']:COL['a_k'] + W_A].reshape(bs, n_new, H_A, 2 * HEAD_DIM))
        outs_s[1].append(pjs[:, :n_new, COL['a_v']:COL['a_v'] + W_A].reshape(bs, n_new, H_A, 2 * HEAD_DIM))
        outs_s[2].append(pjs[:, :n_new, COL['c_k']:COL['c_k'] + W_C].reshape(bs, n_new, H_C, HEAD_DIM))
        outs_s[3].append(pjs[:, :n_new, COL['c_v']:COL['c_v'] + W_C].reshape(bs, n_new, H_C, HEAD_DIM))
        outs_s[4].append(logf_s[:, :n_new, :H_C])
        outs_s[5].append(_state_from_blockdiag(s_new))

    y_p = xp.reshape(bp, seq, d_model)
    y_s = xs.reshape(bs, ROWS_S, d_model)[:, :n_new]
    p_a = [s.reshape(depth, bp, seq, H_A, 2 * HEAD_DIM) for s in a_stacks]
    p_c = [jnp.transpose(s.reshape(depth, bp, H_C, HEAD_DIM, seq), (0, 1, 4, 2, 3)) for s in c_stacks]
    return (y_p, y_s, *p_a, *p_c, *(jnp.stack(o) for o in outs_p), *(jnp.stack(o) for o in outs_s))
```

```python
import functools
import math

import jax
import jax.numpy as jnp
import numpy as np
from jax import lax
from jax.experimental import pallas as pl
from jax.experimental.pallas import tpu as pltpu

F32 = jnp.float32
BF16 = jnp.bfloat16

HEAD_DIM = 64
H_A, H_B, H_C, H_M = 4, 6, 6, 4
W_A, W_B, W_C, W_M = H_A * 2 * HEAD_DIM, H_B * HEAD_DIM, H_C * HEAD_DIM, H_M * HEAD_DIM
MIX = W_A + W_B + W_C + W_M
LANES = 128
SUBLANES = 8
RET_CHUNK = 128
LN_EPS = 1e-5
NEG_INF = -1e30
QK_SCALE = HEAD_DIM ** -0.5
VMEM_LIMIT = 48 * 1024 * 1024

_REF_GROUPS = (('a_q', W_A), ('a_k', W_A), ('a_v', W_A), ('a_g', W_A),
               ('b_q', W_B), ('b_k', W_B), ('b_v', W_B), ('b_g', W_B),
               ('c_q', W_C), ('c_k', W_C), ('c_v', W_C), ('c_g', W_C), ('c_f', H_C),
               ('m_q', W_M), ('m_g', W_M))
_GROUPS = (('a_q', W_A), ('a_k', W_A), ('a_v', W_A), ('a_g', W_A), ('m_q', W_M),
           ('c_q', W_C), ('c_g', W_C), ('b_q', W_B), ('b_k', W_B), ('b_v', W_B), ('b_g', W_B),
           ('m_g', W_M), ('c_f', LANES), ('c_k', W_C), ('c_v', W_C))
REF_COL, COL = {}, {}
_o = 0
for _n, _w in _REF_GROUPS:
    REF_COL[_n] = (_o, _w)
    _o += _w
_o = 0
for _n, _w in _GROUPS:
    assert _o % _w == 0 and _o % LANES == 0
    COL[_n] = _o
    _o += _w
PW = _o
PW_MAIN = COL['c_k']
BLK = {n: o // LANES for n, o in COL.items()}
LOG2E = math.log2(math.e)


def _permute_w_in(w_in):
    parts = []
    for n, w in _GROUPS:
        o, wr = REF_COL[n]
        parts.append(w_in[:, :, o:o + wr])
        if w > wr:
            parts.append(jnp.zeros(w_in.shape[:2] + (w - wr,), w_in.dtype))
    return jnp.concatenate(parts, axis=-1)


def _cparams(sem):
    return pltpu.CompilerParams(dimension_semantics=sem, vmem_limit_bytes=VMEM_LIMIT)


def _silu(x):
    return x / (1.0 + jnp.exp(-x))


def _dot(a, b):
    return jnp.dot(a.astype(BF16), b.astype(BF16), preferred_element_type=F32)


def _dot_nt(a, b):
    return lax.dot_general(a.astype(BF16), b.astype(BF16), (((1,), (1,)), ((), ())),
                           preferred_element_type=F32)


def _lane_half_masks(shape):
    lane = lax.broadcasted_iota(jnp.int32, shape, len(shape) - 1)
    lo = (lane % LANES) < HEAD_DIM
    return lo, jnp.logical_not(lo)


def _proj_kernel(x_ref, w_ref, o_ref, xb_ref):
    @pl.when(pl.program_id(1) == 0)
    def _():
        xb_ref[...] = x_ref[...].astype(BF16)

    o_ref[...] = jnp.dot(xb_ref[...], w_ref[...], preferred_element_type=F32)


def _proj(x2d, w_all, layer, tm, tn, n_cols=None):
    m, k = x2d.shape
    n = w_all.shape[2] if n_cols is None else n_cols
    return pl.pallas_call(
        _proj_kernel,
        out_shape=jax.ShapeDtypeStruct((m, n), F32),
        grid=(m // tm, n // tn),
        in_specs=[pl.BlockSpec((tm, k), lambda i, j: (i, 0)),
                  pl.BlockSpec((None, k, tn), lambda i, j: (layer, 0, j))],
        out_specs=pl.BlockSpec((tm, tn), lambda i, j: (i, j)),
        scratch_shapes=[pltpu.VMEM((tm, k), BF16)],
        compiler_params=_cparams(("parallel", "arbitrary")),
        name="proj",
    )(x2d, w_all)


def _proj_main_kernel(x_ref, w_ref, *rest):
    o_ref, ak_ref, av_ref, xb_ref = rest[-4:]
    first = pl.program_id(1) == 0

    @pl.when(first)
    def _():
        xb_ref[...] = x_ref[...].astype(BF16)

    o = jnp.dot(xb_ref[...], w_ref[...], preferred_element_type=F32)
    o_ref[...] = o

    @pl.when(first)
    def _():
        tm = o.shape[0]
        for h in range(H_A):
            rows = pl.ds(h, tm, stride=H_A)
            ak_ref[rows, :] = o[:, COL['a_k'] + h * LANES:COL['a_k'] + (h + 1) * LANES]
            av_ref[rows, :] = o[:, COL['a_v'] + h * LANES:COL['a_v'] + (h + 1) * LANES]


def _proj_main(x2d, w_all, layer, tm, tn, stacks):
    m, k = x2d.shape
    depth = w_all.shape[0]
    assert tn >= COL['a_v'] + W_A and COL['a_k'] < COL['a_v']
    stack = jax.ShapeDtypeStruct((depth, m * H_A, LANES), F32)
    stack_spec = pl.BlockSpec((None, tm * H_A, LANES), lambda i, j: (layer, i, 0))
    return pl.pallas_call(
        _proj_main_kernel,
        out_shape=(jax.ShapeDtypeStruct((m, PW_MAIN), F32), stack, stack),
        grid=(m // tm, PW_MAIN // tn),
        in_specs=[pl.BlockSpec((tm, k), lambda i, j: (i, 0)),
                  pl.BlockSpec((None, k, tn), lambda i, j: (layer, 0, j))]
                 + [pl.BlockSpec(memory_space=pl.ANY)] * len(stacks),
        out_specs=(pl.BlockSpec((tm, tn), lambda i, j: (i, j)), stack_spec, stack_spec),
        input_output_aliases={2: 1, 3: 2} if stacks else {},
        scratch_shapes=[pltpu.VMEM((tm, k), BF16)],
        compiler_params=_cparams(("parallel", "arbitrary")),
        name="proj_main",
    )(x2d, w_all, *stacks)


def _proj_t_kernel(wt_ref, x_ref, *rest):
    ck_ref, cv_ref = rest[-2:]
    o = _dot_nt(wt_ref[...], x_ref[...])
    ck_ref[...] = o[:W_C]
    cv_ref[...] = o[W_C:]


def _proj_t(x3d, wt_all, layer, tm, stacks):
    b, seq, k = x3d.shape
    depth = wt_all.shape[0]
    stack = jax.ShapeDtypeStruct((depth, b, W_C, seq), F32)
    stack_spec = pl.BlockSpec((None, None, W_C, tm), lambda bb, i: (layer, bb, 0, i))
    return pl.pallas_call(
        _proj_t_kernel,
        out_shape=(stack, stack),
        grid=(b, seq // tm),
        in_specs=[pl.BlockSpec((None, 2 * W_C, k), lambda bb, i: (layer, 0, 0)),
                  pl.BlockSpec((None, tm, k), lambda bb, i: (bb, i, 0))]
                 + [pl.BlockSpec(memory_space=pl.ANY)] * len(stacks),
        out_specs=(stack_spec, stack_spec),
        input_output_aliases={2: 0, 3: 1} if stacks else {},
        compiler_params=_cparams(("parallel", "parallel")),
        name="proj_t",
    )(wt_all, x3d, *stacks)


def _log_sigmoid(z):
    return jnp.minimum(z, 0.0) - jnp.log1p(jnp.exp(-jnp.abs(z)))


def _lane_cumsum(x):
    n = x.shape[-1]
    lane = lax.broadcasted_iota(jnp.int32, x.shape, x.ndim - 1)
    shift = 1
    while shift < n:
        x = x + jnp.where(lane >= shift, pltpu.roll(x, shift, x.ndim - 1), 0.0)
        shift *= 2
    return x


def _logf_cum_kernel(cf_ref, bf_ref, logf_ref, cum_ref):
    lf = _log_sigmoid(cf_ref[...] + bf_ref[...])
    lf_t = lf.T[:SUBLANES, :]
    logf_ref[...] = lf_t
    cum_ref[...] = _lane_cumsum(lf_t)


def _logf_cum(proj3, bf_pad):
    b, seq, _ = proj3.shape
    out = jax.ShapeDtypeStruct((b, SUBLANES, seq), F32)
    return pl.pallas_call(
        _logf_cum_kernel,
        out_shape=(out, out),
        grid=(b,),
        in_specs=[pl.BlockSpec((None, seq, LANES), lambda i: (i, 0, BLK['c_f'])),
                  pl.BlockSpec((1, LANES), lambda i: (0, 0))],
        out_specs=(pl.BlockSpec((None, SUBLANES, seq), lambda i: (i, 0, 0)),
                   pl.BlockSpec((None, SUBLANES, seq), lambda i: (i, 0, 0))),
        compiler_params=_cparams(("parallel",)),
        name="logf_cum",
    )(proj3, bf_pad)


def _tri_pairs(n_q, ratio):
    qi = np.array([i for i in range(n_q) for _ in range((i + 1) * ratio)], np.int32)
    kj = np.array([j for i in range(n_q) for j in range((i + 1) * ratio)], np.int32)
    return jnp.asarray(qi), jnp.asarray(kj)


SOFTMAX_ROWS = 32


def _stack_queries(q_ref, q2_ref, tq):
    q = q_ref[...] * (QK_SCALE * LOG2E)
    lo, hi = _lane_half_masks(q.shape)
    q2_ref[0:tq, :] = jnp.where(lo, q, 0.0).astype(BF16)
    q2_ref[tq:, :] = jnp.where(hi, q, 0.0).astype(BF16)


def _flash_init(m_ref, l_ref, acc_ref):
    m_ref[...] = jnp.full(m_ref.shape, NEG_INF, F32)
    l_ref[...] = jnp.zeros(l_ref.shape, F32)
    acc_ref[...] = jnp.zeros(acc_ref.shape, F32)


def _softmax_tile(s_ref, p_ref, m_ref, l_ref, alpha_ref, col_bias, tq, q_minus_k):
    tk = s_ref.shape[1]
    rc = SOFTMAX_ROWS
    reps = tk // LANES
    causal = q_minus_k is not None
    if causal:
        col = lax.broadcasted_iota(jnp.int32, (rc, tk), 1)
        row0 = lax.broadcasted_iota(jnp.int32, (rc, tk), 0)

    def first_visible(c):
        return (c * rc) % tq + q_minus_k

    chunks = [c for c in range(2 * tq // rc) if not causal or first_visible(c) + rc - 1 >= 0]

    def scores(c):
        s = s_ref[c * rc:(c + 1) * rc, :] + col_bias(c * rc // tq)
        if causal and first_visible(c) < tk - 1:
            s = jnp.where(col <= row0 + first_visible(c), s, NEG_INF)
        return s

    for c in chunks:
        rows = slice(c * rc, (c + 1) * rc)
        m_old = m_ref[rows, :]
        m_new = jnp.maximum(m_old, jnp.max(scores(c), axis=1, keepdims=True))
        alpha_ref[rows, :] = jnp.exp2(m_old - m_new)
        m_ref[rows, :] = m_new
    for c in chunks:
        rows = slice(c * rc, (c + 1) * rc)
        p = jnp.exp2(scores(c) - jnp.concatenate([m_ref[rows, :]] * reps, axis=1))
        l_ref[rows, :] = alpha_ref[rows, :] * l_ref[rows, :] + jnp.sum(p, axis=1, keepdims=True)
        p_ref[rows, :] = p.astype(BF16)


def _live_rows(tq, q_minus_k):
    r0 = 0 if q_minus_k is None else max(0, -q_minus_k)
    return [slice(0, 2 * tq)] if r0 == 0 else [slice(r0, tq), slice(tq + r0, 2 * tq)]


def _flash_steps(i, j, tq, tk, init, step, finish):
    ratio = tq // tk

    @pl.when(j == 0)
    def _():
        init()

    @pl.when(j < i * ratio)
    def _():
        step(None)

    for t in range(ratio):
        @pl.when(j == i * ratio + t)
        def _():
            step(-t * tk)

    @pl.when(j == (i + 1) * ratio - 1)
    def _():
        finish()


def _lambda(lamv_ref, lam_init):
    lv = lamv_ref[...]
    e1 = jnp.exp(jnp.sum(lv[0:1] * lv[1:2], axis=1, keepdims=True))
    e2 = jnp.exp(jnp.sum(lv[2:3] * lv[3:4], axis=1, keepdims=True))
    return e1 - e2 + lam_init


def _attn_a_kernel(qi_ref, kj_ref, slopes_ref, lamv_ref, g_ref, q_ref, k_ref, v_ref, gate_ref,
                   o_ref, q2_ref, s_ref, p_ref, m_ref, l_ref, alpha_ref, acc_ref, *, tq, tk, lam_init):
    h = pl.program_id(1)
    p = pl.program_id(2)
    i = qi_ref[p]
    j = kj_ref[p]

    def init():
        _stack_queries(q_ref, q2_ref, tq)
        _flash_init(m_ref, l_ref, acc_ref)

    def step(q_minus_k):
        live = _live_rows(tq, q_minus_k)
        for rows in live:
            s_ref[rows, :] = _dot_nt(q2_ref[rows, :], k_ref[...])
        kpos = lax.broadcasted_iota(jnp.int32, (1, tk), 1) + j * tk
        bias = (slopes_ref[h] * LOG2E) * kpos.astype(F32)
        _softmax_tile(s_ref, p_ref, m_ref, l_ref, alpha_ref, lambda half: bias, tq, q_minus_k)
        for rows in live:
            acc_ref[rows, :] = alpha_ref[rows, :] * acc_ref[rows, :] + _dot(p_ref[rows, :], v_ref[...])

    def finish():
        o = acc_ref[...] / l_ref[...]
        o = o[:tq] - _lambda(lamv_ref, lam_init) * o[tq:]
        y = o * lax.rsqrt(jnp.mean(o * o, axis=1, keepdims=True) + LN_EPS) * g_ref[...]
        o_ref[...] = y * (1.0 - lam_init) * _silu(gate_ref[...])

    _flash_steps(i, j, tq, tk, init, step, finish)


def _attn_a(proj3, slopes, lamv, g, lam_init, tq, tk):
    b, seq, _ = proj3.shape
    qi, kj = _tri_pairs(seq // tq, tq // tk)

    def blk(col, rows, tile):
        return pl.BlockSpec((None, rows, LANES),
                            lambda bb, h, p, qi, kj: (bb, (qi if tile == 'q' else kj)[p], col + h))

    kern = functools.partial(_attn_a_kernel, tq=tq, tk=tk, lam_init=lam_init)
    return pl.pallas_call(
        kern,
        out_shape=jax.ShapeDtypeStruct((b, seq, W_A), F32),
        grid_spec=pltpu.PrefetchScalarGridSpec(
            num_scalar_prefetch=2,
            grid=(b, H_A, qi.shape[0]),
            in_specs=[pl.BlockSpec(memory_space=pltpu.SMEM),
                      pl.BlockSpec((4, HEAD_DIM), lambda bb, h, p, qi, kj: (0, 0)),
                      pl.BlockSpec((1, LANES), lambda bb, h, p, qi, kj: (0, 0)),
                      blk(BLK['a_q'], tq, 'q'), blk(BLK['a_k'], tk, 'k'), blk(BLK['a_v'], tk, 'k'),
                      blk(BLK['a_g'], tq, 'q')],
            out_specs=pl.BlockSpec((None, tq, LANES), lambda bb, h, p, qi, kj: (bb, qi[p], h)),
            scratch_shapes=_flash_scratch(tq, tk)),
        compiler_params=_cparams(("parallel", "parallel", "arbitrary")),
        name="attn_a",
    )(qi, kj, slopes, lamv, g, proj3, proj3, proj3, proj3)


def _flash_scratch(tq, tk):
    rep = pltpu.VMEM((2 * tq, LANES), F32)
    return [pltpu.VMEM((2 * tq, LANES), BF16),
            pltpu.VMEM((2 * tq, tk), F32),
            pltpu.VMEM((2 * tq, tk), BF16),
            rep, rep, rep, rep]


def _attn_c_kernel(qi_ref, kj_ref, q_ref, kt_ref, vt_ref, cum_ref, gate_ref,
                   o_ref, q2_ref, s_ref, p_ref, m_ref, l_ref, alpha_ref, acc_ref, *, tq, tk):
    hb = pl.program_id(1)
    p = pl.program_id(2)
    i = qi_ref[p]
    j = kj_ref[p]

    def init():
        _stack_queries(q_ref, q2_ref, tq)
        _flash_init(m_ref, l_ref, acc_ref)

    def step(q_minus_k):
        live = _live_rows(tq, q_minus_k)
        for rows in live:
            s_ref[rows, :] = _dot(q2_ref[rows, :], kt_ref[...])
        bias = [cum_ref[pl.ds(2 * hb + half, 1), :] * (-LOG2E) for half in range(2)]
        _softmax_tile(s_ref, p_ref, m_ref, l_ref, alpha_ref, lambda half: bias[half], tq, q_minus_k)
        for rows in live:
            acc_ref[rows, :] = alpha_ref[rows, :] * acc_ref[rows, :] + _dot_nt(p_ref[rows, :], vt_ref[...])

    def finish():
        o = acc_ref[...] / l_ref[...]
        lo, _ = _lane_half_masks((tq, LANES))
        o_ref[...] = jnp.where(lo, o[:tq], o[tq:]) * _silu(gate_ref[...])

    _flash_steps(i, j, tq, tk, init, step, finish)


def _attn_c(proj3, kt_all, vt_all, layer, cum_t, tq, tk):
    b, seq, _ = proj3.shape
    qi, kj = _tri_pairs(seq // tq, tq // tk)
    n_pairs = H_C // 2

    def blk(col):
        return pl.BlockSpec((None, tq, LANES), lambda bb, h, p, qi, kj: (bb, qi[p], col + h))

    blk_t = pl.BlockSpec((None, None, LANES, tk), lambda bb, h, p, qi, kj: (layer, bb, h, kj[p]))

    return pl.pallas_call(
        functools.partial(_attn_c_kernel, tq=tq, tk=tk),
        out_shape=jax.ShapeDtypeStruct((b, seq, W_C), F32),
        grid_spec=pltpu.PrefetchScalarGridSpec(
            num_scalar_prefetch=2,
            grid=(b, n_pairs, qi.shape[0]),
            in_specs=[blk(BLK['c_q']), blk_t, blk_t,
                      pl.BlockSpec((None, SUBLANES, tk), lambda bb, h, p, qi, kj: (bb, 0, kj[p])),
                      blk(BLK['c_g'])],
            out_specs=pl.BlockSpec((None, tq, LANES), lambda bb, h, p, qi, kj: (bb, qi[p], h)),
            scratch_shapes=_flash_scratch(tq, tk)),
        compiler_params=_cparams(("parallel", "parallel", "arbitrary")),
        name="attn_c",
    )(qi, kj, proj3, kt_all, vt_all, cum_t, proj3)


def _ret_tables(chunk, n_valid):
    log_gamma = jnp.log(1.0 - 2.0 ** (-5.0 - jnp.arange(H_B, dtype=F32)))
    idx = jnp.arange(chunk, dtype=F32)
    valid = idx < n_valid
    rel = idx[:, None] - idx[None, :]
    decay = jnp.where(rel >= 0, jnp.exp(log_gamma[:, None, None] * jnp.maximum(rel, 0.0)), 0.0)
    key_rows = max(chunk, LANES)
    decay = jnp.pad(decay, ((0, 0), (0, 0), (0, key_rows - chunk)))
    k_w = jnp.where(valid, jnp.exp(log_gamma[:, None] * (n_valid - 1 - idx)[None, :]), 0.0)
    q_w = jnp.exp(log_gamma[:, None] * (idx + 1)[None, :])
    g = jnp.exp(log_gamma * n_valid)

    def lanes(t):
        t = t.reshape(H_B // 2, 2, chunk)
        return jnp.repeat(jnp.transpose(t, (0, 2, 1)), HEAD_DIM, axis=2)

    g_rows = jnp.repeat(g.reshape(H_B // 2, 2), HEAD_DIM, axis=1)
    g_tab = jnp.broadcast_to(g_rows[:, :, None], (H_B // 2, LANES, LANES))
    return decay.reshape(H_B // 2, 2, chunk, key_rows), lanes(k_w) * QK_SCALE, lanes(q_w), g_tab


def _retention_kernel(q_ref, k_ref, v_ref, gate_ref, s0_ref, decay_ref, kw_ref, qw_ref, g_ref,
                      o_ref, s_out_ref, s_ref, *, chunk, n_chunks):
    t = pl.program_id(2)

    @pl.when(t == 0)
    def _():
        s_ref[...] = s0_ref[...]

    shape = (chunk, LANES)
    lo, hi = _lane_half_masks(shape)
    r = lax.broadcasted_iota(jnp.int32, (LANES, LANES), 0)
    c = lax.broadcasted_iota(jnp.int32, (LANES, LANES), 1)
    same_head = (r < HEAD_DIM) == (c < HEAD_DIM)

    for n in range(n_chunks):
        rows = slice(n * chunk, (n + 1) * chunk)
        q = q_ref[rows, :]
        k = k_ref[rows, :]
        v = v_ref[rows, :]
        kw = k * kw_ref[...]
        if chunk < LANES:
            pad = jnp.zeros((LANES - chunk, LANES), F32)
            k, v, kw = (jnp.concatenate([a, pad], axis=0) for a in (k, v, kw))
        lo_k, hi_k = _lane_half_masks(v.shape)
        state = s_ref[...]
        o_cross = _dot(q * qw_ref[...], state)
        a_lo = _dot_nt(jnp.where(lo, q, 0.0), k) * (QK_SCALE * decay_ref[0])
        a_hi = _dot_nt(jnp.where(hi, q, 0.0), k) * (QK_SCALE * decay_ref[1])
        o = o_cross + _dot(a_lo, jnp.where(lo_k, v, 0.0)) + _dot(a_hi, jnp.where(hi_k, v, 0.0))
        kv = _dot(kw.T, v)
        s_ref[...] = g_ref[...] * state + jnp.where(same_head, kv, 0.0)
        sq = o * o
        ms_lo = jnp.sum(jnp.where(lo, sq, 0.0), axis=1, keepdims=True) * (1.0 / HEAD_DIM)
        ms_hi = jnp.sum(jnp.where(hi, sq, 0.0), axis=1, keepdims=True) * (1.0 / HEAD_DIM)
        y = o * lax.rsqrt(jnp.where(lo, ms_lo, ms_hi) + LN_EPS)
        o_ref[rows, :] = y * _silu(gate_ref[rows, :])

    @pl.when(t == pl.num_programs(2) - 1)
    def _():
        s_out_ref[...] = s_ref[...]


def _retention(proj3, s0_bd, tables, chunk, rows_per_step):
    b, seq, _ = proj3.shape
    decay, kw, qw, g_tab = tables
    n_pairs = H_B // 2
    n_chunks = rows_per_step // chunk

    def blk(col):
        return pl.BlockSpec((None, rows_per_step, LANES), lambda bb, h, t: (bb, t, col + h))

    def tab(shape):
        return pl.BlockSpec((None,) + shape, lambda bb, h, t: (h,) + (0,) * len(shape))

    return pl.pallas_call(
        functools.partial(_retention_kernel, chunk=chunk, n_chunks=n_chunks),
        out_shape=(jax.ShapeDtypeStruct((b, seq, W_B), F32),
                   jax.ShapeDtypeStruct((b, n_pairs, LANES, LANES), F32)),
        grid=(b, n_pairs, seq // rows_per_step),
        in_specs=[blk(BLK['b_q']), blk(BLK['b_k']), blk(BLK['b_v']), blk(BLK['b_g']),
                  pl.BlockSpec((None, None, LANES, LANES), lambda bb, h, t: (bb, h, 0, 0)),
                  tab((2, chunk, max(chunk, LANES))), tab((chunk, LANES)), tab((chunk, LANES)),
                  tab((LANES, LANES))],
        out_specs=(pl.BlockSpec((None, rows_per_step, LANES), lambda bb, h, t: (bb, t, h)),
                   pl.BlockSpec((None, None, LANES, LANES), lambda bb, h, t: (bb, h, 0, 0))),
        scratch_shapes=[pltpu.VMEM((LANES, LANES), F32)],
        compiler_params=_cparams(("parallel", "parallel", "arbitrary")),
        name="retention",
    )(proj3, proj3, proj3, proj3, s0_bd, decay, kw, qw, g_tab)


def _state_to_blockdiag(s):
    b = s.shape[0]
    s = s.reshape(b, H_B // 2, 2, HEAD_DIM, HEAD_DIM)
    z = jnp.zeros_like(s[:, :, 0])
    top = jnp.concatenate([s[:, :, 0], z], axis=-1)
    bot = jnp.concatenate([z, s[:, :, 1]], axis=-1)
    return jnp.concatenate([top, bot], axis=-2)


def _state_from_blockdiag(s):
    b = s.shape[0]
    even = s[:, :, :HEAD_DIM, :HEAD_DIM]
    odd = s[:, :, HEAD_DIM:, HEAD_DIM:]
    return jnp.stack([even, odd], axis=2).reshape(b, H_B, HEAD_DIM, HEAD_DIM)


def _mem_attn_kernel(q_ref, gate_ref, mk_ref, mv_ref, o_ref):
    q = q_ref[...] * QK_SCALE
    lo, hi = _lane_half_masks((q.shape[0], LANES))
    outs = []
    for pair in range(H_M // 2):
        cols = slice(pair * LANES, (pair + 1) * LANES)
        qp = q[:, cols]
        mk = mk_ref[:, cols]
        mv = mv_ref[:, cols]
        o_pair = None
        for mask in (lo, hi):
            s = _dot_nt(jnp.where(mask, qp, 0.0), mk)
            p = jnp.exp(s - jnp.max(s, axis=1, keepdims=True))
            o = _dot(p, mv) / jnp.sum(p, axis=1, keepdims=True)
            o_pair = jnp.where(mask, o, 0.0) if o_pair is None else jnp.where(mask, o, o_pair)
        outs.append(o_pair)
    o_ref[...] = jnp.concatenate(outs, axis=1) * _silu(gate_ref[...])


def _mem_attn_t_kernel(q_ref, gate_ref, mkt_ref, mvt_ref, o_ref):
    q = q_ref[...] * QK_SCALE
    col = lax.broadcasted_iota(jnp.int32, q.shape, 1)
    o_all = None
    for h in range(H_M):
        head = col // HEAD_DIM == h
        s = _dot(jnp.where(head, q, 0.0), mkt_ref[...])
        p = jnp.exp(s - jnp.max(s, axis=1, keepdims=True))
        o = _dot_nt(p, mvt_ref[...]) / jnp.sum(p, axis=1, keepdims=True)
        o_all = jnp.where(head, o, 0.0) if o_all is None else jnp.where(head, o, o_all)
    o_ref[...] = o_all * _silu(gate_ref[...])


def _mem_attn(proj3, mk_arr, mv_arr, mk_blk, mv_blk, tq, transposed, mem_off=0):
    b, seq, _ = proj3.shape
    mem_block = (None, W_M, mk_arr.shape[2]) if transposed else (None, mk_arr.shape[1], W_M)
    return pl.pallas_call(
        _mem_attn_t_kernel if transposed else _mem_attn_kernel,
        out_shape=jax.ShapeDtypeStruct((b, seq, W_M), F32),
        grid=(b, seq // tq),
        in_specs=[pl.BlockSpec((None, tq, W_M), lambda bb, i: (bb, i, COL['m_q'] // W_M)),
                  pl.BlockSpec((None, tq, W_M), lambda bb, i: (bb, i, COL['m_g'] // W_M)),
                  pl.BlockSpec(mem_block, lambda bb, i: (bb + mem_off, 0, mk_blk)),
                  pl.BlockSpec(mem_block, lambda bb, i: (bb + mem_off, 0, mv_blk))],
        out_specs=pl.BlockSpec((None, tq, W_M), lambda bb, i: (bb, i, 0)),
        compiler_params=_cparams(("parallel", "parallel")),
        name="mem_attn",
    )(proj3, proj3, mk_arr, mv_arr)


def _out_ln_kernel(oa_ref, ob_ref, oc_ref, om_ref, w_ref, x_ref, g_ref, b_ref, y_ref, *, alpha):
    y = _dot(oa_ref[...], w_ref[0:W_A, :])
    y += _dot(ob_ref[...], w_ref[W_A:W_A + W_B, :])
    y += _dot(oc_ref[...], w_ref[W_A + W_B:W_A + W_B + W_C, :])
    y += _dot(om_ref[...], w_ref[W_A + W_B + W_C:, :])
    z = alpha * x_ref[...] + y
    mu = jnp.mean(z, axis=1, keepdims=True)
    zc = z - mu
    var = jnp.mean(zc * zc, axis=1, keepdims=True)
    y_ref[...] = zc * lax.rsqrt(var + LN_EPS) * g_ref[...] + b_ref[...]


def _out_ln(o_a, o_b, o_c, o_m, w_out_all, layer, x2d, ln_g, ln_b, alpha, tm):
    m, d = x2d.shape

    def rows(w):
        return pl.BlockSpec((tm, w), lambda i: (i, 0))

    return pl.pallas_call(
        functools.partial(_out_ln_kernel, alpha=alpha),
        out_shape=jax.ShapeDtypeStruct((m, d), F32),
        grid=(m // tm,),
        in_specs=[rows(W_A), rows(W_B), rows(W_C), rows(W_M),
                  pl.BlockSpec((None, MIX, d), lambda i: (layer, 0, 0)),
                  rows(d),
                  pl.BlockSpec((None, 1, d), lambda i: (layer, 0, 0)),
                  pl.BlockSpec((None, 1, d), lambda i: (layer, 0, 0))],
        out_specs=rows(d),
        compiler_params=_cparams(("parallel",)),
        name="out_ln",
    )(o_a, o_b, o_c, o_m, w_out_all, x2d, ln_g, ln_b)


PAGES_PER_STEP = 32
ROWS_S = 8


def _split3(x):
    hi = x.astype(BF16).astype(F32)
    mid = (x - hi).astype(BF16).astype(F32)
    lo = x - hi - mid
    return hi, mid, lo


def _online_merge(s, m_ref, l_ref):
    m_old = m_ref[...]
    m_new = jnp.maximum(m_old, jnp.max(s, axis=1, keepdims=True))
    alpha = jnp.exp(m_old - m_new)
    p = jnp.exp(s - m_new)
    l_ref[...] = alpha * l_ref[...] + jnp.sum(p, axis=1, keepdims=True)
    m_ref[...] = m_new
    return alpha, p


def _rows_from_tokens(x, n_new):
    return jnp.concatenate([jnp.broadcast_to(x[t:t + 1], (SUBLANES, x.shape[1])) for t in range(n_new)],
                           axis=0)


def _slot_sum(x, n_new):
    return jnp.sum(x.reshape(n_new, SUBLANES, x.shape[1]), axis=1)


def _dec_a_kernel(pt_ref, slopes_ref, lamv_ref, g_ref, q_ref, kn_ref, vn_ref, gate_ref, *rest,
                  n_new, n_past, page, lam_init):
    k_pages = rest[:PAGES_PER_STEP]
    v_pages = rest[PAGES_PER_STEP:2 * PAGES_PER_STEP]
    o_ref, q_rows_ref, m_ref, l_ref, acc_ref = rest[2 * PAGES_PER_STEP:]
    g = pl.program_id(1)
    n_rows = n_new * SUBLANES
    row = lax.broadcasted_iota(jnp.int32, (n_rows, 1), 0)
    t_row = row // SUBLANES
    h_row = (row % SUBLANES) // 2
    slope_row = jnp.zeros((n_rows, 1), F32)
    for h in range(H_A):
        slope_row = jnp.where(h_row == h, slopes_ref[h], slope_row)

    def head_rows(x):
        out = jnp.zeros((n_rows, LANES), F32)
        for h in range(H_A):
            out = jnp.where(h_row == h, _rows_from_tokens(x[:, h * LANES:(h + 1) * LANES], n_new), out)
        return out

    @pl.when(g == 0)
    def _():
        q = head_rows(q_ref[...] * QK_SCALE)
        lane = lax.broadcasted_iota(jnp.int32, q.shape, 1)
        q_rows_ref[...] = jnp.where(lane // HEAD_DIM == row % 2, q, 0.0)
        _flash_init(m_ref, l_ref, acc_ref)

    q_rows = q_rows_ref[...]
    col = lax.broadcasted_iota(jnp.int32, (n_rows, page * H_A), 1)
    own = (col % H_A) == h_row
    rel = (col // H_A - (n_past + t_row)).astype(F32)
    bias0 = jnp.where(own, slope_row * rel, NEG_INF)
    s_parts = []
    for n, kp in enumerate(k_pages):
        base = ((g * PAGES_PER_STEP + n) * page).astype(F32)
        s_parts.append(_dot_nt(q_rows, kp[...]) + (bias0 + slope_row * base))
    alpha, p = _online_merge(jnp.concatenate(s_parts, axis=1), m_ref, l_ref)
    acc = alpha * acc_ref[...]
    w = page * H_A
    for n, vp in enumerate(v_pages):
        acc += _dot(p[:, n * w:(n + 1) * w], vp[...])
    acc_ref[...] = acc

    @pl.when(g == pl.num_programs(1) - 1)
    def _():
        kn = kn_ref[...]
        vn = vn_ref[...]
        s_new = []
        for u in range(n_new):
            ku = head_rows(jnp.broadcast_to(kn[u:u + 1], (n_new, W_A)))
            su = jnp.sum(q_rows * ku, axis=1, keepdims=True) + slope_row * (u - t_row).astype(F32)
            s_new.append(jnp.where(t_row >= u, su, NEG_INF))
        alpha, p = _online_merge(jnp.concatenate(s_new, axis=1), m_ref, l_ref)
        acc = alpha * acc_ref[...]
        for u in range(n_new):
            acc += p[:, u:u + 1] * head_rows(jnp.broadcast_to(vn[u:u + 1], (n_new, W_A)))
        o = acc / l_ref[...]
        o = o * jnp.where(row % 2 == 0, 1.0, -_lambda(lamv_ref, lam_init))
        outs = []
        for h in range(H_A):
            oh = _slot_sum(jnp.where(h_row == h, o, 0.0), n_new)
            yh = oh * lax.rsqrt(jnp.mean(oh * oh, axis=1, keepdims=True) + LN_EPS) * g_ref[...]
            outs.append(yh * (1.0 - lam_init))
        y = jnp.concatenate(outs, axis=1) * _silu(gate_ref[...][:n_new])
        o_ref[...] = jnp.concatenate([y, jnp.zeros((ROWS_S - n_new, W_A), F32)], axis=0)


def _dec_a(proj3, cache_k, cache_v, layer, pt_flat, slopes, lamv, g, lam_init, n_new):
    b = proj3.shape[0]
    page = cache_k.shape[2] // H_A
    n_pages = pt_flat.shape[0] // b
    n_steps = n_pages // PAGES_PER_STEP

    def new_blk(col):
        return pl.BlockSpec((None, ROWS_S, W_A), lambda bb, s, pt: (bb, 0, col // W_A))

    def page_blk(n):
        return pl.BlockSpec((None, None, page * H_A, LANES),
                            lambda bb, s, pt: (layer, pt[bb * n_pages + s * PAGES_PER_STEP + n], 0, 0))

    n_rows = n_new * SUBLANES
    kern = functools.partial(_dec_a_kernel, n_new=n_new, n_past=n_pages * page, page=page,
                             lam_init=lam_init)
    return pl.pallas_call(
        kern,
        out_shape=jax.ShapeDtypeStruct((b, ROWS_S, W_A), F32),
        grid_spec=pltpu.PrefetchScalarGridSpec(
            num_scalar_prefetch=1,
            grid=(b, n_steps),
            in_specs=[pl.BlockSpec(memory_space=pltpu.SMEM),
                      pl.BlockSpec((4, HEAD_DIM), lambda bb, s, pt: (0, 0)),
                      pl.BlockSpec((1, LANES), lambda bb, s, pt: (0, 0)),
                      new_blk(COL['a_q']), new_blk(COL['a_k']), new_blk(COL['a_v']),
                      new_blk(COL['a_g'])]
                     + [page_blk(n) for n in range(PAGES_PER_STEP)] * 2,
            out_specs=pl.BlockSpec((None, ROWS_S, W_A), lambda bb, s, pt: (bb, 0, 0)),
            scratch_shapes=[pltpu.VMEM((n_rows, LANES), F32),
                            pltpu.VMEM((n_rows, 1), F32),
                            pltpu.VMEM((n_rows, 1), F32),
                            pltpu.VMEM((n_rows, LANES), F32)]),
        compiler_params=_cparams(("parallel", "arbitrary")),
        name="dec_a",
    )(pt_flat, slopes, lamv, g, proj3, proj3, proj3, proj3,
      *([cache_k] * PAGES_PER_STEP), *([cache_v] * PAGES_PER_STEP))


def _dec_c_kernel(pt_ref, bf_ref, q_ref, kn_ref, vn_ref, cf_ref, gate_ref, *rest,
                  n_new, page, n_pages):
    k_pages = rest[:PAGES_PER_STEP]
    v_pages = rest[PAGES_PER_STEP:2 * PAGES_PER_STEP]
    f_pages = rest[2 * PAGES_PER_STEP:3 * PAGES_PER_STEP]
    o_ref, logf_ref, qbd_ref, m_ref, l_ref, acc_ref, run_ref = rest[3 * PAGES_PER_STEP:]
    b = pl.program_id(0)
    g = pl.program_id(1)
    n_rows = n_new * SUBLANES
    row = lax.broadcasted_iota(jnp.int32, (n_rows, 1), 0)
    t_row = row // SUBLANES
    h_row = row % SUBLANES

    @pl.when(g == 0)
    def _():
        qr = _rows_from_tokens(q_ref[...] * QK_SCALE, n_new)
        col = lax.broadcasted_iota(jnp.int32, (n_rows, W_C), 1)
        qbd_ref[...] = jnp.where(col // HEAD_DIM == h_row, qr, 0.0)
        _flash_init(m_ref, l_ref, acc_ref)
        run_ref[...] = jnp.zeros(run_ref.shape, F32)

    qbd = qbd_ref[...]
    n_r = PAGES_PER_STEP * SUBLANES
    lf = jnp.concatenate([fp[...] for fp in f_pages], axis=0)
    tok_r = lax.broadcasted_iota(jnp.int32, (page, page), 0)
    tok_c = lax.broadcasted_iota(jnp.int32, (page, page), 1)
    y = _dot(jnp.concatenate(_split3(lf), axis=0), jnp.where(tok_r <= tok_c, 1.0, 0.0))
    cum = y[:n_r] + y[n_r:2 * n_r] + y[2 * n_r:]
    r = lax.broadcasted_iota(jnp.int32, (n_r, n_r), 0)
    c = lax.broadcasted_iota(jnp.int32, (n_r, n_r), 1)
    earlier = jnp.where((r % SUBLANES == c % SUBLANES) & (c < r), 1.0, 0.0)
    tot = jnp.broadcast_to(cum[:, page - 1:page], (n_r, LANES))
    y = _dot(earlier, jnp.concatenate(_split3(tot), axis=1))
    off = y[:, :LANES] + y[:, LANES:2 * LANES] + y[:, 2 * LANES:]
    cum = cum + off + jnp.concatenate([run_ref[...]] * PAGES_PER_STEP, axis=0)
    run = cum[n_r - SUBLANES:, page - 1:page]
    run_ref[...] = run
    s_parts = []
    for n, kp in enumerate(k_pages):
        cum_n = cum[n * SUBLANES:(n + 1) * SUBLANES]
        s_parts.append(_dot(qbd, kp[...]) - jnp.concatenate([cum_n] * n_new, axis=0))
    alpha, p = _online_merge(jnp.concatenate(s_parts, axis=1), m_ref, l_ref)
    acc = alpha * acc_ref[...]
    for n, vp in enumerate(v_pages):
        acc += _dot_nt(p[:, n * page:(n + 1) * page], vp[...])
    acc_ref[...] = acc

    @pl.when(g == pl.num_programs(1) - 1)
    def _():
        kn = kn_ref[...]
        vn = vn_ref[...]
        lf = _log_sigmoid(cf_ref[...] + bf_ref[...])
        logf_ref[...] = lf
        lane = lax.broadcasted_iota(jnp.int32, (n_rows, LANES), 1)
        pick = lane == h_row
        run_rows = jnp.concatenate([run] * n_new, axis=0)
        cum_new = jnp.zeros((1, LANES), F32)
        s_new = []
        for u in range(n_new):
            cum_new = cum_new + lf[u:u + 1]
            cu = jnp.sum(jnp.where(pick, cum_new, 0.0), axis=1, keepdims=True)
            su = jnp.sum(qbd * kn[u:u + 1], axis=1, keepdims=True) - (run_rows + cu)
            s_new.append(jnp.where(t_row >= u, su, NEG_INF))
        alpha, p = _online_merge(jnp.concatenate(s_new, axis=1), m_ref, l_ref)
        acc = alpha * acc_ref[...]
        for u in range(n_new):
            acc += p[:, u:u + 1] * vn[u:u + 1]
        o = acc / l_ref[...]
        col = lax.broadcasted_iota(jnp.int32, o.shape, 1)
        o = _slot_sum(jnp.where(col // HEAD_DIM == h_row, o, 0.0), n_new)
        y = o * _silu(gate_ref[...][:n_new])
        o_ref[...] = jnp.concatenate([y, jnp.zeros((ROWS_S - n_new, W_C), F32)], axis=0)


def _dec_c(proj3, cache_kt, cache_vt, cache_f, layer, pt_flat, bf_pad, n_new):
    b = proj3.shape[0]
    page = cache_kt.shape[3]
    n_pages = pt_flat.shape[0] // b
    n_steps = n_pages // PAGES_PER_STEP

    def new_blk(col, w):
        return pl.BlockSpec((None, ROWS_S, w), lambda bb, s, pt: (bb, 0, col // w))

    def page_idx(bb, s, pt, n):
        return pt[bb * n_pages + s * PAGES_PER_STEP + n]

    def kv_blk(n):
        return pl.BlockSpec((None, None, W_C, page),
                            lambda bb, s, pt: (layer, page_idx(bb, s, pt, n), 0, 0))

    def f_blk(n):
        return pl.BlockSpec((None, None, SUBLANES, page),
                            lambda bb, s, pt: (layer, page_idx(bb, s, pt, n), 0, 0))

    n_rows = n_new * SUBLANES
    return pl.pallas_call(
        functools.partial(_dec_c_kernel, n_new=n_new, page=page, n_pages=n_pages),
        out_shape=(jax.ShapeDtypeStruct((b, ROWS_S, W_C), F32),
                   jax.ShapeDtypeStruct((b, ROWS_S, LANES), F32)),
        grid_spec=pltpu.PrefetchScalarGridSpec(
            num_scalar_prefetch=1,
            grid=(b, n_steps),
            in_specs=[pl.BlockSpec((1, LANES), lambda bb, s, pt: (0, 0)),
                      new_blk(COL['c_q'], W_C), new_blk(COL['c_k'], W_C), new_blk(COL['c_v'], W_C),
                      new_blk(COL['c_f'], LANES), new_blk(COL['c_g'], W_C)]
                     + [kv_blk(n) for n in range(PAGES_PER_STEP)] * 2
                     + [f_blk(n) for n in range(PAGES_PER_STEP)],
            out_specs=(pl.BlockSpec((None, ROWS_S, W_C), lambda bb, s, pt: (bb, 0, 0)),
                       pl.BlockSpec((None, ROWS_S, LANES), lambda bb, s, pt: (bb, 0, 0))),
            scratch_shapes=[pltpu.VMEM((n_rows, W_C), F32),
                            pltpu.VMEM((n_rows, 1), F32),
                            pltpu.VMEM((n_rows, 1), F32),
                            pltpu.VMEM((n_rows, W_C), F32),
                            pltpu.VMEM((SUBLANES, 1), F32)]),
        compiler_params=_cparams(("parallel", "arbitrary")),
        name="dec_c",
    )(pt_flat, bf_pad, proj3, proj3, proj3, proj3, proj3,
      *([cache_kt] * PAGES_PER_STEP), *([cache_vt] * PAGES_PER_STEP), *([cache_f] * PAGES_PER_STEP))


def _pick_tile(n, prefs):
    for t in prefs:
        if n % t == 0:
            return t
    return n


def kernel(x_prompt, x_sample, cache_a_k, cache_a_v, cache_c_k, cache_c_v, cache_c_logf, cache_mem_k, cache_mem_v, state_ret, page_table, mem_prompt, w_in, b_f, lam_q1, lam_k1, lam_q2, lam_k2, a_subln_g, w_out, w_mem_kv, ln_g, ln_b):
    depth, d_model, _ = w_in.shape
    bp, seq, _ = x_prompt.shape
    bs, n_new, _ = x_sample.shape
    n_phys, page = cache_a_k.shape[1], cache_a_k.shape[2]
    n_mem = mem_prompt.shape[1]
    alpha = (2 * depth) ** 0.25
    assert n_new <= ROWS_S and page_table.shape[1] % PAGES_PER_STEP == 0
    assert page == LANES

    w_perm = _permute_w_in(w_in).astype(BF16)
    ck0 = REF_COL['c_k'][0]
    assert REF_COL['c_v'][0] == ck0 + W_C
    wt_ckv = jnp.swapaxes(w_in[:, :, ck0:ck0 + 2 * W_C], 1, 2).astype(BF16)
    w_out_b = w_out.astype(BF16)
    w_mem_b = w_mem_kv.astype(BF16)
    bf_pad = jnp.pad(b_f, ((0, 0), (0, LANES - H_C)))[:, None, :]
    lamv = jnp.stack([lam_q1, lam_k1, lam_q2, lam_k2], axis=1)
    g_a = a_subln_g[:, None, :]
    ln_g3, ln_b3 = ln_g[:, None, :], ln_b[:, None, :]
    slopes = 2.0 ** (-8.0 * jnp.arange(1, H_A + 1, dtype=F32) / H_A)
    ck_a = cache_a_k.reshape(depth, n_phys, page * H_A, 2 * HEAD_DIM)
    cv_a = cache_a_v.reshape(depth, n_phys, page * H_A, 2 * HEAD_DIM)
    ck_c = jnp.transpose(cache_c_k, (0, 1, 3, 4, 2)).reshape(depth, n_phys, W_C, page)
    cv_c = jnp.transpose(cache_c_v, (0, 1, 3, 4, 2)).reshape(depth, n_phys, W_C, page)
    cf_c = jnp.pad(jnp.transpose(cache_c_logf, (0, 1, 3, 2)),
                   ((0, 0), (0, 0), (0, SUBLANES - H_C), (0, 0)))
    pt_flat = page_table.reshape(-1)
    mem2d = mem_prompt.reshape(bp * n_mem, d_model)
    smk = jnp.transpose(cache_mem_k, (0, 1, 3, 4, 2)).reshape(depth * bs, W_M, n_mem)
    smv = jnp.transpose(cache_mem_v, (0, 1, 3, 4, 2)).reshape(depth * bs, W_M, n_mem)

    chunk_p = RET_CHUNK if seq % RET_CHUNK == 0 else seq
    tables_p = _ret_tables(chunk_p, chunk_p)
    tables_s = _ret_tables(ROWS_S, n_new)
    s0_p = jnp.zeros((bp, H_B // 2, LANES, LANES), F32)

    tq = _pick_tile(seq, (1024, 512, 256, 128))
    tk = _pick_tile(tq, (512, 256, 128))
    tm_p = _pick_tile(bp * seq, (1024, 512, 256, 128))
    tn = _pick_tile(PW, (1152, 640, 128))
    tn_main = _pick_tile(PW_MAIN, (1664, 384, 128))
    tm_t = _pick_tile(seq, (1024, 512, 256, 128))
    ret_rows = _pick_tile(seq, (1024, 512, 256, 128))
    tq_m = _pick_tile(seq, (1024, 512, 256, 128))

    xp = x_prompt.reshape(bp * seq, d_model)
    xs = jnp.pad(x_sample, ((0, 0), (0, ROWS_S - n_new), (0, 0))).reshape(bs * ROWS_S, d_model)

    outs_p = [[] for _ in range(4)]
    a_stacks, c_stacks = [], []
    outs_s = [[] for _ in range(6)]
    for l in range(depth):
        lam_init = 0.8 - 0.6 * math.exp(-0.3 * l)

        pj, *a_stacks = _proj_main(xp, w_perm, l, tm_p, tn_main, a_stacks)
        pj = pj.reshape(bp, seq, PW_MAIN)
        c_stacks = _proj_t(xp.reshape(bp, seq, d_model), wt_ckv, l, tm_t, c_stacks)
        logf_t, cum_t = _logf_cum(pj, bf_pad[l])
        o_a = _attn_a(pj, slopes, lamv[l], g_a[l], lam_init, tq, tk)
        o_c = _attn_c(pj, c_stacks[0], c_stacks[1], l, cum_t, tq, tk)
        o_b, s_new = _retention(pj, s0_p, tables_p, chunk_p, ret_rows)
        mkv = _proj(mem2d, w_mem_b, l, _pick_tile(bp * n_mem, (512, 256, 128)), W_M).reshape(bp, n_mem, 2 * W_M)
        o_m = _mem_attn(pj, mkv, mkv, 0, 1, tq_m, False)
        xp = _out_ln(o_a.reshape(bp * seq, W_A), o_b.reshape(bp * seq, W_B), o_c.reshape(bp * seq, W_C),
                     o_m.reshape(bp * seq, W_M), w_out_b, l, xp, ln_g3, ln_b3, alpha,
                     _pick_tile(bp * seq, (512, 256, 128)))
        outs_p[0].append(jnp.transpose(logf_t[:, :H_C, :], (0, 2, 1)))
        outs_p[1].append(_state_from_blockdiag(s_new))
        outs_p[2].append(mkv[:, :, :W_M].reshape(bp, n_mem, H_M, HEAD_DIM))
        outs_p[3].append(mkv[:, :, W_M:].reshape(bp, n_mem, H_M, HEAD_DIM))

        pjs = _proj(xs, w_perm, l, _pick_tile(bs * ROWS_S, (256, 128, 8)), tn).reshape(bs, ROWS_S, PW)
        o_a = _dec_a(pjs, ck_a, cv_a, l, pt_flat, slopes, lamv[l], g_a[l], lam_init, n_new)
        o_c, logf_s = _dec_c(pjs, ck_c, cv_c, cf_c, l, pt_flat, bf_pad[l], n_new)
        o_b, s_new = _retention(pjs, _state_to_blockdiag(state_ret[l]), tables_s, ROWS_S, ROWS_S)
        o_m = _mem_attn(pjs, smk, smv, 0, 0, ROWS_S, True, mem_off=l * bs)
        xs = _out_ln(o_a.reshape(bs * ROWS_S, W_A), o_b.reshape(bs * ROWS_S, W_B),
                     o_c.reshape(bs * ROWS_S, W_C), o_m.reshape(bs * ROWS_S, W_M), w_out_b, l, xs,
                     ln_g3, ln_b3, alpha, _pick_tile(bs * ROWS_S, (256, 128, 8)))
        outs_s[0].append(pjs[:, :n_new, COL['a_k']:COL['a_k'] + W_A].reshape(bs, n_new, H_A, 2 * HEAD_DIM))
        outs_s[1].append(pjs[:, :n_new, COL['a_v']:COL['a_v'] + W_A].reshape(bs, n_new, H_A, 2 * HEAD_DIM))
        outs_s[2].append(pjs[:, :n_new, COL['c_k']:COL['c_k'] + W_C].reshape(bs, n_new, H_C, HEAD_DIM))
        outs_s[3].append(pjs[:, :n_new, COL['c_v']:COL['c_v'] + W_C].reshape(bs, n_new, H_C, HEAD_DIM))
        outs_s[4].append(logf_s[:, :n_new, :H_C])
        outs_s[5].append(_state_from_blockdiag(s_new))

    y_p = xp.reshape(bp, seq, d_model)
    y_s = xs.reshape(bs, ROWS_S, d_model)[:, :n_new]
    p_a = [s.reshape(depth, bp, seq, H_A, 2 * HEAD_DIM) for s in a_stacks]
    p_c = [jnp.transpose(s.reshape(depth, bp, H_C, HEAD_DIM, seq), (0, 1, 4, 2, 3)) for s in c_stacks]
    return (y_p, y_s, *p_a, *p_c, *(jnp.stack(o) for o in outs_p), *(jnp.stack(o) for o in outs_s))
```
